```python
import math
import jax, jax.numpy as jnp
from jax import lax
import numpy as np

D_MODEL = 1024
BATCH = 8
SEQ = 4096
DEPTH = 4

NSA_HEADS = 8
NSA_KV_GROUPS = 2
NSA_HPG = NSA_HEADS // NSA_KV_GROUPS
NSA_HEAD_DIM = 64
N_BRANCH = 3
CMP_BLOCK = 32
CMP_STRIDE = 16
CMP_HIDDEN = 256
SLC_BLOCK = 64
SLC_TOP_N = 16
WINDOW = 512
NSA_Q_BLOCK = 64
ROPE_THETA = 500000.0
ROPE_DIM = NSA_HEAD_DIM // 4

GDN_HEADS = 4
GDN_HEAD_DIM = 128
GDN_CONV = 4
GDN_CHUNK = 64

NSA_WIDTH = NSA_HEADS * NSA_HEAD_DIM
GDN_WIDTH = GDN_HEADS * GDN_HEAD_DIM
MIX_WIDTH = NSA_WIDTH + GDN_WIDTH
NSA_KV_WIDTH = N_BRANCH * 2 * NSA_KV_GROUPS * NSA_HEAD_DIM
NSA_GATE_WIDTH = NSA_HEADS * N_BRANCH
IN_SPLITS = (NSA_WIDTH, NSA_KV_WIDTH, NSA_GATE_WIDTH, 3 * GDN_WIDTH, GDN_HEADS, GDN_HEADS, GDN_WIDTH)
IN_WIDTH = sum(IN_SPLITS)

D_FF = 2816
FFN_CONV = 3
NORM_EPS = 1e-6

kernel_name = "hybrid_nsa_gdn_convffn_trunk"


def rms_norm(x, gain):
    xf = x.astype(jnp.float32)
    y = xf * lax.rsqrt(jnp.mean(xf * xf, axis=-1, keepdims=True) + NORM_EPS)
    return (y * gain.astype(jnp.float32)).astype(x.dtype)


def l2_norm(x):
    xf = x.astype(jnp.float32)
    return xf * lax.rsqrt(jnp.sum(xf * xf, axis=-1, keepdims=True) + NORM_EPS)


def split_columns(t, sizes):
    bounds = np.cumsum(sizes)[:-1].tolist()
    return jnp.split(t, bounds, axis=-1)


def rope_tables(positions):
    inv = ROPE_THETA ** (-jnp.arange(0, ROPE_DIM, 2, dtype=jnp.float32) / ROPE_DIM)
    ang = positions.astype(jnp.float32)[..., None] * inv
    return jnp.cos(ang), jnp.sin(ang)


def apply_partial_rope(x, cos, sin):
    half = ROPE_DIM // 2
    x1 = x[..., :half].astype(jnp.float32)
    x2 = x[..., half:ROPE_DIM].astype(jnp.float32)
    c = cos[:, :, None, :]
    s = sin[:, :, None, :]
    rot = jnp.concatenate([x1 * c - x2 * s, x2 * c + x1 * s], axis=-1).astype(x.dtype)
    return jnp.concatenate([rot, x[..., ROPE_DIM:]], axis=-1)


def masked_softmax(scores, mask):
    s = jnp.where(mask, scores.astype(jnp.float32), -jnp.inf)
    m = jnp.max(s, axis=-1, keepdims=True)
    m = jnp.where(jnp.isfinite(m), m, 0.0)
    p = jnp.exp(s - m)
    return p / jnp.maximum(jnp.sum(p, axis=-1, keepdims=True), 1e-30)


def causal_dwconv(x, w):
    width, T = w.shape[0], x.shape[1]
    xp = jnp.pad(x, ((0, 0), (width - 1, 0), (0, 0)))
    out = xp[:, 0:T] * w[0]
    for j in range(1, width):
        out = out + xp[:, j:j + T] * w[j]
    return out


def compress_blocks(x, pe, w1, w2):
    B, G, T, DH = x.shape
    n_cmp = (T - CMP_BLOCK) // CMP_STRIDE + 1
    idx = jnp.arange(n_cmp)[:, None] * CMP_STRIDE + jnp.arange(CMP_BLOCK)[None, :]
    blocks = (x[:, :, idx] + pe).reshape(B, G, n_cmp, CMP_BLOCK * DH)
    return jax.nn.silu(blocks @ w1) @ w2


def nsa_mixer(q, kv, gate_logits, cos, sin, q_gain, k_gain, cmp_pe, cmp_w1, cmp_w2, out_gain):
    B, T, _ = q.shape
    dt = q.dtype
    G, HPG, DH = NSA_KV_GROUPS, NSA_HPG, NSA_HEAD_DIM
    q = q.reshape(B, T, NSA_HEADS, DH)
    q = apply_partial_rope(rms_norm(q, q_gain), cos, sin) * (DH ** -0.5)
    q = q.reshape(B, T, G, HPG, DH).transpose(0, 2, 3, 1, 4)
    gates = jax.nn.sigmoid(gate_logits).reshape(B, T, G, HPG, N_BRANCH).transpose(0, 2, 3, 1, 4)
    kv = kv.reshape(B, T, N_BRANCH, 2, G, DH)

    def heads_first(t):
        return t.transpose(0, 2, 1, 3)

    k_cmp = rms_norm(compress_blocks(heads_first(apply_partial_rope(kv[:, :, 0, 0], cos, sin)),
                                     cmp_pe[0], cmp_w1[0], cmp_w2[0]), k_gain[0])
    v_cmp = compress_blocks(heads_first(kv[:, :, 0, 1]), cmp_pe[1], cmp_w1[1], cmp_w2[1])
    k_slc = heads_first(apply_partial_rope(rms_norm(kv[:, :, 1, 0], k_gain[1]), cos, sin))
    v_slc = heads_first(kv[:, :, 1, 1])
    k_win = heads_first(apply_partial_rope(rms_norm(kv[:, :, 2, 0], k_gain[2]), cos, sin))
    v_win = heads_first(kv[:, :, 2, 1])

    n_cmp = k_cmp.shape[2]
    n_slc = T // SLC_BLOCK
    n_top = min(SLC_TOP_N, n_slc)
    cmp_start = jnp.arange(n_cmp) * CMP_STRIDE
    cmp_end = cmp_start + CMP_BLOCK - 1
    slc_start = jnp.arange(n_slc) * SLC_BLOCK
    overlap = ((cmp_start[:, None] < slc_start[None, :] + SLC_BLOCK)
               & (cmp_start[:, None] + CMP_BLOCK > slc_start[None, :])).astype(jnp.float32)
    k_slc_blocks = k_slc.reshape(B, G, n_slc, SLC_BLOCK, DH)
    v_slc_blocks = v_slc.reshape(B, G, n_slc, SLC_BLOCK, DH)
    k_win_pad = jnp.pad(k_win, ((0, 0), (0, 0), (WINDOW, 0), (0, 0)))
    v_win_pad = jnp.pad(v_win, ((0, 0), (0, 0), (WINDOW, 0), (0, 0)))
    b_ix = jnp.arange(B)[:, None, None, None]
    g_ix = jnp.arange(G)[None, :, None, None]
    blk_offsets = jnp.arange(SLC_BLOCK)
    win_offsets = jnp.arange(NSA_Q_BLOCK + WINDOW)

    def block_fn(qb):
        t0 = qb * NSA_Q_BLOCK
        t_idx = t0 + jnp.arange(NSA_Q_BLOCK)
        q_b = lax.dynamic_slice_in_dim(q, t0, NSA_Q_BLOCK, axis=3)
        g_b = lax.dynamic_slice_in_dim(gates, t0, NSA_Q_BLOCK, axis=3)
        p_c = masked_softmax(jnp.einsum('bghqd,bgnd->bghqn', q_b, k_cmp), cmp_end[None, :] <= t_idx[:, None])
        o_c = jnp.einsum('bghqn,bgnd->bghqd', p_c.astype(dt), v_cmp)
        imp = jnp.einsum('bghqn,ns->bgqs', p_c, overlap)
        cur = (t_idx // SLC_BLOCK)[:, None]
        j = jnp.arange(n_slc)[None, :]
        forced = (j == 0) | (j == cur) | (j == cur - 1)
        imp = jnp.where(forced, jnp.inf, jnp.where(j > cur, -jnp.inf, imp))
        _, sel = lax.top_k(imp, n_top)
        k_s = k_slc_blocks[b_ix, g_ix, sel].reshape(B, G, NSA_Q_BLOCK, n_top * SLC_BLOCK, DH)
        v_s = v_slc_blocks[b_ix, g_ix, sel].reshape(B, G, NSA_Q_BLOCK, n_top * SLC_BLOCK, DH)
        pos_s = (sel[..., None] * SLC_BLOCK + blk_offsets).reshape(B, G, NSA_Q_BLOCK, n_top * SLC_BLOCK)
        mask_s = (pos_s <= t_idx[None, None, :, None])[:, :, None]
        p_s = masked_softmax(jnp.einsum('bghqd,bgqkd->bghqk', q_b, k_s), mask_s)
        o_s = jnp.einsum('bghqk,bgqkd->bghqd', p_s.astype(dt), v_s)
        k_w = lax.dynamic_slice_in_dim(k_win_pad, t0, NSA_Q_BLOCK + WINDOW, axis=2)
        v_w = lax.dynamic_slice_in_dim(v_win_pad, t0, NSA_Q_BLOCK + WINDOW, axis=2)
        pos_w = t0 - WINDOW + win_offsets
        rel = t_idx[:, None] - pos_w[None, :]
        mask_w = (rel >= 0) & (rel < WINDOW) & (pos_w[None, :] >= 0)
        p_w = masked_softmax(jnp.einsum('bghqd,bgkd->bghqk', q_b, k_w), mask_w)
        o_w = jnp.einsum('bghqk,bgkd->bghqd', p_w.astype(dt), v_w)
        return g_b[..., 0:1] * o_c + g_b[..., 1:2] * o_s + g_b[..., 2:3] * o_w

    out = lax.map(block_fn, jnp.arange(T // NSA_Q_BLOCK))
    out = out.transpose(1, 0, 4, 2, 3, 5).reshape(B, T, NSA_HEADS, DH)
    return rms_norm(out, out_gain).reshape(B, T, NSA_WIDTH)


def gated_delta_rule_chunked(q, k, v, g, beta):
    B, T, H, DK = q.shape
    DV = v.shape[-1]
    C = GDN_CHUNK
    N = T // C
    f32 = jnp.float32

    def chunks(t):
        return t.astype(f32).reshape(B, N, C, H, -1).transpose(1, 0, 3, 2, 4)

    qc = chunks(q) * (DK ** -0.5)
    kc = chunks(k)
    vc = chunks(v)
    gc = chunks(g[..., None])[..., 0]
    bc = chunks(beta[..., None])[..., 0]
    gcum = jnp.cumsum(gc, axis=-1)
    causal = jnp.tril(jnp.ones((C, C), bool))
    strict = jnp.tril(jnp.ones((C, C), bool), -1)
    decay = jnp.exp(jnp.where(causal, gcum[..., :, None] - gcum[..., None, :], -jnp.inf))
    kb = kc * bc[..., None]
    vb = vc * bc[..., None]
    a_kk = jnp.where(strict, jnp.einsum('nbhid,nbhjd->nbhij', kb, kc) * decay, 0.0)
    eye = jnp.eye(C, dtype=f32)
    t_inv = lax.linalg.triangular_solve(a_kk + eye, jnp.broadcast_to(eye, a_kk.shape),
                                        left_side=True, lower=True, unit_diagonal=True)
    u = jnp.einsum('nbhij,nbhjd->nbhid', t_inv, vb)
    w = jnp.einsum('nbhij,nbhjd->nbhid', t_inv, kb * jnp.exp(gcum)[..., None])
    a_qk = jnp.einsum('nbhid,nbhjd->nbhij', qc, kc) * decay

    def step(S, xs):
        q_i, k_i, u_i, w_i, a_i, g_i = xs
        v_new = u_i - jnp.einsum('bhcd,bhde->bhce', w_i, S)
        o_i = (jnp.einsum('bhcd,bhde->bhce', q_i * jnp.exp(g_i)[..., None], S)
               + jnp.einsum('bhij,bhje->bhie', a_i, v_new))
        g_last = g_i[..., -1]
        S = (S * jnp.exp(g_last)[..., None, None]
             + jnp.einsum('bhcd,bhce->bhde', k_i * jnp.exp(g_last[..., None] - g_i)[..., None], v_new))
        return S, o_i

    S0 = jnp.zeros((B, H, DK, DV), f32)
    _, o = lax.scan(step, S0, (qc, kc, u, w, a_qk, gcum))
    return o.transpose(1, 0, 3, 2, 4).reshape(B, T, H, DV)


def gdn_mixer(qkv, a_in, b_in, z, conv_w, a_log, dt_bias, out_gain):
    B, T, _ = qkv.shape
    dt = qkv.dtype
    qkv = jax.nn.silu(causal_dwconv(qkv, conv_w))
    q, k, v = jnp.split(qkv, 3, axis=-1)
    q = l2_norm(q.reshape(B, T, GDN_HEADS, GDN_HEAD_DIM))
    k = l2_norm(k.reshape(B, T, GDN_HEADS, GDN_HEAD_DIM))
    v = v.reshape(B, T, GDN_HEADS, GDN_HEAD_DIM)
    beta = jax.nn.sigmoid(b_in.astype(jnp.float32))
    g = -jnp.exp(a_log.astype(jnp.float32)) * jax.nn.softplus(a_in.astype(jnp.float32) + dt_bias.astype(jnp.float32))
    o = gated_delta_rule_chunked(q, k, v, g, beta)
    o = rms_norm(o, out_gain) * jax.nn.silu(z.astype(jnp.float32).reshape(B, T, GDN_HEADS, GDN_HEAD_DIM))
    return o.astype(dt).reshape(B, T, GDN_WIDTH)


def conv_glu_ffn(h, w_up, conv_w, conv_b, w_down):
    u = causal_dwconv(h @ w_up, conv_w) + conv_b
    gate, up = jnp.split(u, 2, axis=-1)
    return (jax.nn.silu(gate) * up) @ w_down


def setup_inputs(seed: int = 0) -> dict:
    key = jax.random.key(seed)
    ks = jax.random.split(key, 24)
    f32 = jnp.float32

    def nrm(k, shape, scale):
        return scale * jax.random.normal(k, shape, f32)

    def gain(k, shape):
        return 1.0 + 0.02 * jax.random.normal(k, shape, f32)

    res = (2 * DEPTH) ** -0.5
    dt_init = jnp.exp(jax.random.uniform(ks[12], (DEPTH, GDN_HEADS), f32, math.log(1e-3), math.log(1e-1)))
    positions = (jax.random.randint(ks[1], (BATCH, 1), 0, 1024, jnp.int32)
                 + jnp.arange(SEQ, dtype=jnp.int32)[None, :])
    return {
        "x": nrm(ks[0], (BATCH, SEQ, D_MODEL), 1.0),
        "positions": positions,
        "attn_norm": gain(ks[2], (DEPTH, D_MODEL)),
        "w_in": nrm(ks[3], (DEPTH, D_MODEL, IN_WIDTH), D_MODEL ** -0.5),
        "nsa_q_norm": gain(ks[4], (DEPTH, NSA_HEAD_DIM)),
        "nsa_k_norm": gain(ks[5], (DEPTH, N_BRANCH, NSA_HEAD_DIM)),
        "cmp_pe": nrm(ks[6], (DEPTH, 2, CMP_BLOCK, NSA_HEAD_DIM), 0.1),
        "cmp_w1": nrm(ks[7], (DEPTH, 2, CMP_BLOCK * NSA_HEAD_DIM, CMP_HIDDEN), (CMP_BLOCK * NSA_HEAD_DIM) ** -0.5),
        "cmp_w2": nrm(ks[8], (DEPTH, 2, CMP_HIDDEN, NSA_HEAD_DIM), CMP_HIDDEN ** -0.5),
        "nsa_out_norm": gain(ks[9], (DEPTH, NSA_HEADS, NSA_HEAD_DIM)),
        "gdn_conv_w": nrm(ks[10], (DEPTH, GDN_CONV, 3 * GDN_WIDTH), GDN_CONV ** -0.5),
        "gdn_a_log": jnp.log(jax.random.uniform(ks[11], (DEPTH, GDN_HEADS), f32, 1.0, 16.0)),
        "gdn_dt_bias": dt_init + jnp.log(-jnp.expm1(-dt_init)),
        "gdn_out_norm": gain(ks[13], (DEPTH, GDN_HEAD_DIM)),
        "w_out": nrm(ks[14], (DEPTH, MIX_WIDTH, D_MODEL), res * MIX_WIDTH ** -0.5),
        "ffn_norm": gain(ks[15], (DEPTH, D_MODEL)),
        "w_up": nrm(ks[16], (DEPTH, D_MODEL, 2 * D_FF), D_MODEL ** -0.5),
        "ffn_conv_w": nrm(ks[17], (DEPTH, FFN_CONV, 2 * D_FF), FFN_CONV ** -0.5),
        "ffn_conv_b": nrm(ks[18], (DEPTH, 2 * D_FF), 0.01),
        "w_down": nrm(ks[19], (DEPTH, D_FF, D_MODEL), res * D_FF ** -0.5),
    }


def reference(x, positions, attn_norm, w_in, nsa_q_norm, nsa_k_norm, cmp_pe, cmp_w1, cmp_w2, nsa_out_norm,
              gdn_conv_w, gdn_a_log, gdn_dt_bias, gdn_out_norm, w_out, ffn_norm, w_up, ffn_conv_w,
              ffn_conv_b, w_down):
    cos, sin = rope_tables(positions)
    for l in range(DEPTH):
        h = rms_norm(x, attn_norm[l])
        proj = h @ w_in[l]
        nsa_q, nsa_kv, nsa_g, gdn_qkv, gdn_a, gdn_b, gdn_z = split_columns(proj, IN_SPLITS)
        o_nsa = nsa_mixer(nsa_q, nsa_kv, nsa_g, cos, sin, nsa_q_norm[l], nsa_k_norm[l],
                          cmp_pe[l], cmp_w1[l], cmp_w2[l], nsa_out_norm[l])
        o_gdn = gdn_mixer(gdn_qkv, gdn_a, gdn_b, gdn_z, gdn_conv_w[l], gdn_a_log[l],
                          gdn_dt_bias[l], gdn_out_norm[l])
        x = x + jnp.concatenate([o_nsa, o_gdn], axis=-1) @ w_out[l]
        h = rms_norm(x, ffn_norm[l])
        x = x + conv_glu_ffn(h, w_up[l], ffn_conv_w[l], ffn_conv_b[l], w_down[l])
    return x
```

```python
import functools

import numpy as np
import jax
import jax.numpy as jnp
from jax import lax
from jax.experimental import pallas as pl
from jax.experimental.pallas import tpu as pltpu

D_MODEL = 1024
NSA_HEADS = 8
NSA_KV_GROUPS = 2
NSA_HPG = NSA_HEADS // NSA_KV_GROUPS
NSA_HEAD_DIM = 64
N_BRANCH = 3
CMP_BLOCK = 32
CMP_STRIDE = 16
CMP_HIDDEN = 256
SLC_BLOCK = 64
SLC_TOP_N = 16
WINDOW = 512
ROPE_THETA = 500000.0
ROPE_DIM = NSA_HEAD_DIM // 4
GDN_HEADS = 4
GDN_HEAD_DIM = 128
GDN_CONV = 4
GDN_CHUNK = 64
NSA_WIDTH = NSA_HEADS * NSA_HEAD_DIM
GDN_WIDTH = GDN_HEADS * GDN_HEAD_DIM
NSA_KV_WIDTH = N_BRANCH * 2 * NSA_KV_GROUPS * NSA_HEAD_DIM
NSA_GATE_WIDTH = NSA_HEADS * N_BRANCH
D_FF = 2816
FFN_CONV = 3
NORM_EPS = 1e-6

LANES = 128
SMALL_WIDTH = LANES
GDN_A_COL = NSA_GATE_WIDTH
GDN_B_COL = NSA_GATE_WIDTH + GDN_HEADS
VMEM_LIMIT = 56 * 1024 * 1024
NEG = -1e30

F32 = jnp.float32
BF16 = jnp.bfloat16


def _cparams(*sem):
    return pltpu.CompilerParams(dimension_semantics=sem, vmem_limit_bytes=VMEM_LIMIT)


def _resident(shape):
    nd = len(shape)
    return pl.BlockSpec(shape, lambda *_: (0,) * nd, pipeline_mode=pl.Buffered(1))


def _silu(x):
    return x * (1.0 / (1.0 + jnp.exp(-x)))


def _sigmoid(x):
    return 1.0 / (1.0 + jnp.exp(-x))


def _rope_kernel(pos_ref, inv_ref, c_ref, s1_ref, s2_ref):
    ang = pos_ref[...].astype(F32) * inv_ref[...]
    lane = lax.broadcasted_iota(jnp.int32, ang.shape, 1) % NSA_HEAD_DIM
    half = ROPE_DIM // 2
    cos = jnp.cos(ang)
    sin = jnp.sin(ang)
    c_ref[...] = jnp.where(lane < ROPE_DIM, cos, 1.0)
    s1_ref[...] = jnp.where((lane >= half) & (lane < ROPE_DIM), sin, 0.0)
    s2_ref[...] = jnp.where(lane < half, -sin, 0.0)


def _rope_tables(positions):
    m = positions.size
    tm = min(m, 1024)
    inv = ROPE_THETA ** (-jnp.arange(0, ROPE_DIM, 2, dtype=F32) / ROPE_DIM)
    lane = np.arange(LANES) % NSA_HEAD_DIM
    inv_row = jnp.where(lane < ROPE_DIM, inv[lane % (ROPE_DIM // 2)], 0.0).reshape(1, LANES)
    tab = jax.ShapeDtypeStruct((m, LANES), F32)
    return pl.pallas_call(
        _rope_kernel,
        grid=(m // tm,),
        in_specs=[pl.BlockSpec((tm, 1), lambda i: (i, 0)), _resident((1, LANES))],
        out_specs=[pl.BlockSpec((tm, LANES), lambda i: (i, 0))] * 3,
        out_shape=[tab] * 3,
        compiler_params=_cparams("parallel"),
        name="rope_tables",
    )(positions.reshape(m, 1), inv_row)


IN_SEG = (NSA_WIDTH, NSA_KV_WIDTH, 3 * GDN_WIDTH, GDN_WIDTH, SMALL_WIDTH)


def _in_proj_kernel(x_ref, g_ref, w_ref, q_ref, kv_ref, gq_ref, z_ref, sm_ref):
    x = x_ref[...]
    h = x * lax.rsqrt(jnp.mean(x * x, axis=-1, keepdims=True) + NORM_EPS) * g_ref[...]
    h = h.astype(BF16)
    off = 0
    for ref, width in zip((q_ref, kv_ref, gq_ref, z_ref, sm_ref), IN_SEG):
        ref[...] = jnp.dot(h, w_ref[0, :, off:off + width], preferred_element_type=F32)
        off += width


def _regroup_w_in(w_in):
    o_q = 0
    o_kv = o_q + NSA_WIDTH
    o_g = o_kv + NSA_KV_WIDTH
    o_gq = o_g + NSA_GATE_WIDTH
    o_a = o_gq + 3 * GDN_WIDTH
    o_b = o_a + GDN_HEADS
    o_z = o_b + GDN_HEADS
    pad = SMALL_WIDTH - NSA_GATE_WIDTH - 2 * GDN_HEADS
    parts = [w_in[..., o_q:o_g], w_in[..., o_gq:o_a], w_in[..., o_z:o_z + GDN_WIDTH],
             w_in[..., o_g:o_gq], w_in[..., o_a:o_z],
             jnp.zeros(w_in.shape[:-1] + (pad,), w_in.dtype)]
    return jnp.concatenate(parts, axis=-1).astype(BF16)


def _in_proj(x2, gain, w_all, layer, tm=512):
    m = x2.shape[0]
    tm = min(tm, m)
    npad = sum(IN_SEG)
    outs = [jax.ShapeDtypeStruct((m, w), F32) for w in IN_SEG]
    return pl.pallas_call(
        _in_proj_kernel,
        grid=(m // tm,),
        in_specs=[pl.BlockSpec((tm, D_MODEL), lambda i: (i, 0)),
                  _resident((1, D_MODEL)),
                  pl.BlockSpec((1, D_MODEL, npad), lambda i: (layer, 0, 0), pipeline_mode=pl.Buffered(1))],
        out_specs=[pl.BlockSpec((tm, w), lambda i: (i, 0)) for w in IN_SEG],
        out_shape=outs,
        compiler_params=_cparams("parallel"),
        name="in_proj",
    )(x2, gain.reshape(1, D_MODEL), w_all)


def _head_rms(x, gain_row):
    lane = lax.broadcasted_iota(jnp.int32, x.shape, 1)
    x2 = x * x
    left = jnp.sum(jnp.where(lane < NSA_HEAD_DIM, x2, 0.0), axis=-1, keepdims=True)
    right = jnp.sum(jnp.where(lane >= NSA_HEAD_DIM, x2, 0.0), axis=-1, keepdims=True)
    ms = jnp.where(lane < NSA_HEAD_DIM, left, right) * (1.0 / NSA_HEAD_DIM)
    return x * lax.rsqrt(ms + NORM_EPS) * gain_row


def _rope(x, c, s1, s2):
    half = ROPE_DIM // 2
    return x * c + pltpu.roll(x, half, 1) * s1 + pltpu.roll(x, LANES - half, 1) * s2


def _nsa_prep_kernel(q_ref, kv_ref, c_ref, s1_ref, s2_ref, qg_ref, kg_ref,
                     qo_ref, kc_ref, kst_ref, vs_ref, kwt_ref, vw_ref):
    c, s1, s2 = c_ref[...], s1_ref[...], s2_ref[...]
    qg = qg_ref[...]
    for j in range(NSA_WIDTH // LANES):
        x = q_ref[0, :, j * LANES:(j + 1) * LANES]
        y = _rope(_head_rms(x, qg), c, s1, s2) * (NSA_HEAD_DIM ** -0.5)
        qo_ref[0, :, j * LANES:(j + 1) * LANES] = y.astype(BF16)
    kv = lambda i: kv_ref[0, :, i * LANES:(i + 1) * LANES]
    kc_ref[0] = _rope(kv(0), c, s1, s2)
    kst_ref[0] = _rope(_head_rms(kv(2), kg_ref[1:2, :]), c, s1, s2).T.astype(BF16)
    vs_ref[0] = kv(3).astype(BF16)
    kwt_ref[0] = _rope(_head_rms(kv(4), kg_ref[2:3, :]), c, s1, s2).T.astype(BF16)
    vw_ref[0] = kv(5).astype(BF16)


def _nsa_prep(q_raw, kv_raw, tabs, q_gain, k_gain, B, T, tm=512):
    tm = min(tm, T)
    nt = T // tm
    c, s1, s2 = tabs
    qg = jnp.tile(q_gain.reshape(1, NSA_HEAD_DIM), (1, 2))
    kg = jnp.tile(k_gain, (1, 2))
    kg = jnp.concatenate([kg, jnp.zeros((5, LANES), F32)], axis=0)
    tok = lambda w: pl.BlockSpec((1, tm, w), lambda b, i: (b, i, 0))
    tab = pl.BlockSpec((tm, LANES), lambda b, i: (b * nt + i, 0))
    tr = pl.BlockSpec((1, LANES, tm), lambda b, i: (b, 0, i))
    return pl.pallas_call(
        _nsa_prep_kernel,
        grid=(B, nt),
        in_specs=[tok(NSA_WIDTH), tok(NSA_KV_WIDTH), tab, tab, tab,
                  _resident((1, LANES)), _resident((8, LANES))],
        out_specs=[tok(NSA_WIDTH), tok(LANES), tr, tok(LANES), tr, tok(LANES)],
        out_shape=[jax.ShapeDtypeStruct((B, T, NSA_WIDTH), BF16),
                   jax.ShapeDtypeStruct((B, T, LANES), F32),
                   jax.ShapeDtypeStruct((B, LANES, T), BF16),
                   jax.ShapeDtypeStruct((B, T, LANES), BF16),
                   jax.ShapeDtypeStruct((B, LANES, T), BF16),
                   jax.ShapeDtypeStruct((B, T, LANES), BF16)],
        compiler_params=_cparams("parallel", "parallel"),
        name="nsa_prep",
    )(q_raw.reshape(B, T, NSA_WIDTH), kv_raw.reshape(B, T, NSA_KV_WIDTH), c, s1, s2, qg, kg)


CMP_ROW = CMP_STRIDE * LANES
CMP_HID2 = NSA_KV_GROUPS * CMP_HIDDEN


def _compress_kernel(r_ref, pe_ref, w1_ref, w2_ref, kg_ref, kt_ref, v_ref):
    is_k = pl.program_id(0) == 0
    r = r_ref[0, 0]
    ha = jnp.dot((r + pe_ref[0, 0:1, :]).astype(BF16), w1_ref[0, 0], preferred_element_type=F32)
    hb = jnp.dot((r + pe_ref[0, 1:2, :]).astype(BF16), w1_ref[0, 1], preferred_element_type=F32)
    n16 = r.shape[0]
    hid = ha + pltpu.roll(hb, n16 - 1, 0)
    y = jnp.dot(_silu(hid).astype(BF16), w2_ref[0], preferred_element_type=F32)
    yk = _head_rms(y, kg_ref[...])
    kt_ref[0] = jnp.where(is_k, yk, y).T.astype(BF16)
    v_ref[0] = y.astype(BF16)


def _compress_weights(cmp_pe, cmp_w1, cmp_w2):
    L = cmp_w1.shape[0]
    G, DH, H = NSA_KV_GROUPS, NSA_HEAD_DIM, CMP_HIDDEN
    w1 = cmp_w1.reshape(L, 2, 2, CMP_STRIDE, DH, H)
    eye = jnp.eye(G, dtype=cmp_w1.dtype)
    w1g = jnp.einsum('lkspdh,gf->lkspgdfh', w1, eye)
    w1g = w1g.reshape(L, 2, 2, CMP_ROW, G * H).astype(BF16)
    w2g = jnp.einsum('lkhd,gf->lkghfd', cmp_w2, eye).reshape(L, 2, G * H, G * DH).astype(BF16)
    pe = cmp_pe.reshape(L, 2, 2, CMP_STRIDE, 1, DH)
    pe = jnp.broadcast_to(pe, (L, 2, 2, CMP_STRIDE, G, DH)).reshape(L, 2, 2, CMP_ROW)
    return pe, w1g, w2g


def _compress(kc_in, kv_raw3, pe, w1g, w2g, k_gain0, layer, B, T):
    n16 = T // CMP_STRIDE
    rk = kc_in.reshape(B, n16, CMP_ROW)
    rv = kv_raw3[:, :, LANES:2 * LANES].reshape(B, n16, CMP_ROW)
    r = jnp.stack([rk, rv], axis=0)
    kg = jnp.tile(k_gain0.reshape(1, NSA_HEAD_DIM), (1, 2))
    return pl.pallas_call(
        _compress_kernel,
        grid=(2, B),
        in_specs=[pl.BlockSpec((1, 1, n16, CMP_ROW), lambda s, b: (s, b, 0, 0)),
                  pl.BlockSpec((1, 2, CMP_ROW), lambda s, b: (layer * 2 + s, 0, 0)),
                  pl.BlockSpec((1, 2, CMP_ROW, CMP_HID2), lambda s, b: (layer * 2 + s, 0, 0, 0)),
                  pl.BlockSpec((1, CMP_HID2, LANES), lambda s, b: (layer * 2 + s, 0, 0)),
                  _resident((1, LANES))],
        out_specs=[pl.BlockSpec((1, LANES, n16), lambda s, b: (s * B + b, 0, 0)),
                   pl.BlockSpec((1, n16, LANES), lambda s, b: (s * B + b, 0, 0))],
        out_shape=[jax.ShapeDtypeStruct((2 * B, LANES, n16), BF16),
                   jax.ShapeDtypeStruct((2 * B, n16, LANES), BF16)],
        compiler_params=_cparams("arbitrary", "parallel"),
        name="compress",
    )(r, pe.reshape(-1, 2, CMP_ROW), w1g.reshape(-1, 2, CMP_ROW, CMP_HID2),
      w2g.reshape(-1, CMP_HID2, LANES), kg)


ATT_TQ = 128
ATT_TK = 256
WIN_KEYS = WINDOW + ATT_TQ


def _softmax_rows(s3, valid):
    s3 = jnp.where(valid, s3, NEG)
    m = jnp.max(s3, axis=-1, keepdims=True)
    p = jnp.where(valid, jnp.exp(s3 - m), 0.0)
    den = jnp.maximum(jnp.sum(p, axis=-1, keepdims=True), 1e-30)
    return p * (1.0 / den)


def _nsa_attn_kernel(q_ref, gt_ref, kct_ref, vc_ref, kst_ref, vs_ref, kwt_ref, vw_ref,
                     ov_ref, e_ref, og_ref, o_ref, *, n_slc, n_cmp_pad):
    TQ, TK, H, DH = ATT_TQ, ATT_TK, NSA_HPG, NSA_HEAD_DIM
    t0 = pl.program_id(1) * TQ
    T = kst_ref.shape[2]
    gt = gt_ref[0]
    t_col = t0 + lax.broadcasted_iota(jnp.int32, (TQ, 1), 0)

    ci = lax.broadcasted_iota(jnp.int32, (TQ, n_cmp_pad), 1)
    valid_c = (ci * CMP_STRIDE + (CMP_BLOCK - 1)) <= t_col

    w0 = pl.multiple_of(jnp.maximum(t0 - WINDOW, 0), LANES)
    rel = t_col - (w0 + lax.broadcasted_iota(jnp.int32, (TQ, WIN_KEYS), 1))
    valid_w = (rel >= 0) & (rel < WINDOW)

    jj = lax.broadcasted_iota(jnp.int32, (n_slc, TQ), 0)
    cur = (t0 + lax.broadcasted_iota(jnp.int32, (n_slc, TQ), 1)) // SLC_BLOCK
    forced = (jj == 0) | (jj == cur) | (jj == cur - 1)

    n_kt = (t0 + TQ + TK - 1) // TK

    for g in range(NSA_KV_GROUPS):
        lo, hi = g * DH, (g + 1) * DH
        qg = q_ref[0, :, g * H * DH:(g + 1) * H * DH]
        qs = jnp.concatenate([qg[:, h * DH:(h + 1) * DH] for h in range(H)], axis=0)

        sc = jnp.dot(qs, kct_ref[0, lo:hi, :], preferred_element_type=F32)
        pc = _softmax_rows(sc.reshape(H, TQ, n_cmp_pad), valid_c[None])
        o_c = jnp.dot(pc.reshape(H * TQ, n_cmp_pad).astype(BF16), vc_ref[0], preferred_element_type=F32)

        psum = pc[0]
        for h in range(1, H):
            psum = psum + pc[h]
        imp = jnp.dot(psum, ov_ref[...], preferred_element_type=F32,
                      precision=lax.Precision.HIGHEST)
        imp_t = imp.T[:n_slc]
        val = jnp.where(forced, jnp.inf, jnp.where(jj > cur, -jnp.inf, imp_t))
        cnt = jnp.zeros((n_slc, TQ), F32)
        for jp in range(n_slc):
            row = val[jp:jp + 1, :]
            wins_tie = jnp.where(row >= val, 1.0, 0.0)
            wins = jnp.where(row > val, 1.0, 0.0)
            cnt = cnt + jnp.where(jj > jp, wins_tie, wins)
        sel_t = ((cnt < float(SLC_TOP_N)) & (jj <= cur)).astype(F32)
        if n_slc < LANES:
            sel_t = jnp.concatenate([sel_t, jnp.zeros((LANES - n_slc, TQ), F32)], axis=0)
        sel = sel_t.T.astype(BF16)

        def slc_body(kt, carry):
            m, l, acc = carry
            k0 = pl.multiple_of(kt * TK, TK)
            s = jnp.dot(qs, kst_ref[0, lo:hi, pl.ds(k0, TK)], preferred_element_type=F32)
            mexp = jnp.dot(sel, e_ref[:, pl.ds(k0, TK)], preferred_element_type=F32)
            pos = k0 + lax.broadcasted_iota(jnp.int32, (TQ, TK), 1)
            bias = jnp.where((mexp > 0.5) & (pos <= t_col), 0.0, NEG)
            s3 = s.reshape(H, TQ, TK) + bias[None]
            m_new = jnp.maximum(m, jnp.max(s3, axis=-1, keepdims=True))
            alpha = jnp.exp(m - m_new)
            p = jnp.exp(s3 - m_new)
            l = alpha * l + jnp.sum(p, axis=-1, keepdims=True)
            pv = jnp.dot(p.reshape(H * TQ, TK).astype(BF16), vs_ref[0, pl.ds(k0, TK), :],
                         preferred_element_type=F32)
            acc = acc * alpha.reshape(H * TQ, 1) + pv
            return m_new, l, acc

        init = (jnp.full((H, TQ, 1), NEG, F32), jnp.zeros((H, TQ, 1), F32),
                jnp.zeros((H * TQ, LANES), F32))
        _, l_s, acc_s = lax.fori_loop(0, n_kt, slc_body, init)
        o_s = acc_s * (1.0 / jnp.maximum(l_s, 1e-30)).reshape(H * TQ, 1)

        sw = jnp.dot(qs, kwt_ref[0, lo:hi, pl.ds(w0, WIN_KEYS)], preferred_element_type=F32)
        pw = _softmax_rows(sw.reshape(H, TQ, WIN_KEYS), valid_w[None])
        o_w = jnp.dot(pw.reshape(H * TQ, WIN_KEYS).astype(BF16), vw_ref[0, pl.ds(w0, WIN_KEYS), :],
                      preferred_element_type=F32)

        def gate(br):
            return jnp.concatenate(
                [gt[:, (g * H + h) * N_BRANCH + br:(g * H + h) * N_BRANCH + br + 1] for h in range(H)], axis=0)

        o = gate(0) * o_c + gate(1) * o_s + gate(2) * o_w
        o = o[:, lo:hi]
        o = o * lax.rsqrt(jnp.mean(o * o, axis=-1, keepdims=True) + NORM_EPS)
        for h in range(H):
            c0 = (g * H + h) * DH
            o_ref[0, :, c0:c0 + DH] = o[h * TQ:(h + 1) * TQ] * og_ref[:, c0:c0 + DH]


def _nsa_consts(T):
    n_cmp = (T - CMP_BLOCK) // CMP_STRIDE + 1
    n_slc = T // SLC_BLOCK
    n_cmp_pad = T // CMP_STRIDE
    cs = np.arange(n_cmp_pad) * CMP_STRIDE
    ss = np.arange(LANES) * SLC_BLOCK
    ov = ((cs[:, None] < ss[None, :] + SLC_BLOCK) & (cs[:, None] + CMP_BLOCK > ss[None, :]))
    ov = ov & (np.arange(n_cmp_pad)[:, None] < n_cmp) & (np.arange(LANES)[None, :] < n_slc)
    e = (np.arange(LANES)[:, None] == (np.arange(T)[None, :] // SLC_BLOCK))
    return jnp.asarray(ov, F32), jnp.asarray(e, BF16), n_slc, n_cmp_pad


def _nsa_attn(q, gates, kct, vc, kst, vs, kwt, vw, out_gain, B, T):
    ov, e, n_slc, n_cmp_pad = _nsa_consts(T)
    assert n_slc <= LANES and T % ATT_TK == 0 and T >= WIN_KEYS
    nq = T // ATT_TQ
    per_b = lambda s1, s2: pl.BlockSpec((1, s1, s2), lambda b, i: (b, 0, 0))
    return pl.pallas_call(
        functools.partial(_nsa_attn_kernel, n_slc=n_slc, n_cmp_pad=n_cmp_pad),
        grid=(B, nq),
        in_specs=[pl.BlockSpec((1, ATT_TQ, NSA_WIDTH), lambda b, i: (b, i, 0)),
                  pl.BlockSpec((1, ATT_TQ, SMALL_WIDTH), lambda b, i: (b, i, 0)),
                  per_b(LANES, n_cmp_pad), per_b(n_cmp_pad, LANES),
                  per_b(LANES, T), per_b(T, LANES), per_b(LANES, T), per_b(T, LANES),
                  _resident((n_cmp_pad, LANES)), _resident((LANES, T)), _resident((1, NSA_WIDTH))],
        out_specs=pl.BlockSpec((1, ATT_TQ, NSA_WIDTH), lambda b, i: (b, i, 0)),
        out_shape=jax.ShapeDtypeStruct((B, T, NSA_WIDTH), F32),
        compiler_params=_cparams("parallel", "parallel"),
        name="nsa_attn",
    )(q, gates, kct, vc, kst, vs, kwt, vw, ov, e, out_gain.reshape(1, NSA_WIDTH))


SUBLANES = 8


def _shift_rows(x, hist, k):
    xr = pltpu.roll(x, k, 0)
    hr = pltpu.roll(hist, k, 0)
    row = lax.broadcasted_iota(jnp.int32, hist.shape, 0)
    top = jnp.where(row < k, hr, xr[:SUBLANES])
    return jnp.concatenate([top, xr[SUBLANES:]], axis=0)


def _causal_conv(x, hist, w_ref, width):
    y = x * w_ref[width - 1:width, :]
    for k in range(1, width):
        y = y + _shift_rows(x, hist, k) * w_ref[width - 1 - k:width - k, :]
    return y


def _gdn_prep_kernel(x_ref, sm_ref, w_ref, alog_ref, dtb_ref,
                     q_ref, k_ref, v_ref, sg_ref, gl_ref, hist_ref):
    @pl.when(pl.program_id(1) == 0)
    def _():
        hist_ref[...] = jnp.zeros_like(hist_ref)

    x = x_ref[0]
    y = _silu(_causal_conv(x, hist_ref[...], w_ref, GDN_CONV))
    hist_ref[...] = x[x.shape[0] - SUBLANES:, :]
    for h in range(GDN_HEADS):
        for src, dst, scale in ((0, q_ref, GDN_HEAD_DIM ** -0.5), (1, k_ref, None)):
            c0 = src * GDN_WIDTH + h * GDN_HEAD_DIM
            t = y[:, c0:c0 + GDN_HEAD_DIM]
            t = t * lax.rsqrt(jnp.sum(t * t, axis=-1, keepdims=True) + NORM_EPS)
            if scale is not None:
                t = t * scale
            dst[0, :, h * GDN_HEAD_DIM:(h + 1) * GDN_HEAD_DIM] = t
    v_ref[0] = y[:, 2 * GDN_WIDTH:]
    sm = sm_ref[0]
    sg_ref[0] = _sigmoid(sm)
    a = sm + dtb_ref[...]
    softplus = jnp.maximum(a, 0.0) + jnp.log(1.0 + jnp.exp(-jnp.abs(a)))
    gl_ref[0] = -jnp.exp(alog_ref[...]) * softplus


def _gdn_prep(gqkv, small, conv_w, a_log, dt_bias, B, T, tm=256):
    tm = min(tm, T)
    w = jnp.concatenate([conv_w, jnp.zeros((SUBLANES - GDN_CONV, 3 * GDN_WIDTH), F32)], axis=0)
    place = lambda v: jnp.zeros((1, SMALL_WIDTH), F32).at[0, GDN_A_COL:GDN_A_COL + GDN_HEADS].set(v)
    tok = lambda c: pl.BlockSpec((1, tm, c), lambda b, i: (b, i, 0))
    return pl.pallas_call(
        _gdn_prep_kernel,
        grid=(B, T // tm),
        in_specs=[tok(3 * GDN_WIDTH), tok(SMALL_WIDTH), _resident((SUBLANES, 3 * GDN_WIDTH)),
                  _resident((1, SMALL_WIDTH)), _resident((1, SMALL_WIDTH))],
        out_specs=[tok(GDN_WIDTH), tok(GDN_WIDTH), tok(GDN_WIDTH), tok(SMALL_WIDTH), tok(SMALL_WIDTH)],
        out_shape=[jax.ShapeDtypeStruct((B, T, GDN_WIDTH), F32)] * 3
                  + [jax.ShapeDtypeStruct((B, T, SMALL_WIDTH), F32)] * 2,
        scratch_shapes=[pltpu.VMEM((SUBLANES, 3 * GDN_WIDTH), F32)],
        compiler_params=_cparams("parallel", "arbitrary"),
        name="gdn_prep",
    )(gqkv.reshape(B, T, 3 * GDN_WIDTH), small.reshape(B, T, SMALL_WIDTH), w, place(a_log), place(dt_bias))


def _dot_nt(a, b, **kw):
    return lax.dot_general(a, b, (((1,), (1,)), ((), ())), preferred_element_type=F32, **kw)


def _dot_tn(a, b, **kw):
    return lax.dot_general(a, b, (((0,), (0,)), ((), ())), preferred_element_type=F32, **kw)


def _unit_lower_inverse(a, ii, jj):
    hp = dict(preferred_element_type=F32, precision=lax.Precision.HIGHEST)
    eye = (ii == jj).astype(F32)
    blk = lambda n: (ii // n) == (jj // n)
    ad = jnp.where(blk(16), a, 0.0)
    t = eye - ad
    p = ad
    for _ in range(3):
        p = jnp.dot(p, p, **hp)
        t = jnp.dot(t, eye + p, **hp)
    for n in (32, 64):
        off = jnp.where(blk(n) & ~blk(n // 2), a, 0.0)
        t = t - jnp.dot(jnp.dot(t, off, **hp), t, **hp)
    return t


def _gdn_scan_kernel(q_ref, k_ref, v_ref, sg_ref, gl_ref, z_ref, og_ref, o_ref, s_ref):
    C, DK = GDN_CHUNK, GDN_HEAD_DIM

    @pl.when(pl.program_id(1) == 0)
    def _():
        s_ref[...] = jnp.zeros_like(s_ref)

    ii = lax.broadcasted_iota(jnp.int32, (C, C), 0)
    jj = lax.broadcasted_iota(jnp.int32, (C, C), 1)
    causal = ii >= jj
    sg = sg_ref[0]
    gl = gl_ref[0]
    for h in range(GDN_HEADS):
        sl = slice(h * DK, (h + 1) * DK)
        qh, kh, vh = q_ref[0, :, sl], k_ref[0, :, sl], v_ref[0, :, sl]
        g_col = gl[:, GDN_A_COL + h:GDN_A_COL + h + 1]
        b_col = sg[:, GDN_B_COL + h:GDN_B_COL + h + 1]
        g_mat = jnp.broadcast_to(g_col, (C, C))
        g_row = jnp.sum(jnp.where(ii == jj, g_mat, 0.0), axis=0, keepdims=True)
        gc_col = jnp.sum(jnp.where(causal, jnp.broadcast_to(g_row, (C, C)), 0.0), axis=1, keepdims=True)
        gc_row = jnp.sum(jnp.where(ii <= jj, g_mat, 0.0), axis=0, keepdims=True)
        decay = jnp.where(causal, jnp.exp(jnp.where(causal, gc_col - gc_row, 0.0)), 0.0)
        kb = kh * b_col
        vb = vh * b_col
        a_kk = jnp.where(ii > jj, _dot_nt(kb, kh) * decay, 0.0)
        t_inv = _unit_lower_inverse(a_kk, ii, jj)
        egc = jnp.exp(gc_col)
        u = jnp.dot(t_inv, vb, preferred_element_type=F32)
        w = jnp.dot(t_inv, kb * egc, preferred_element_type=F32)
        a_qk = _dot_nt(qh, kh) * decay
        s = s_ref[h]
        v_new = u - jnp.dot(w, s, preferred_element_type=F32)
        o = (jnp.dot(qh * egc, s, preferred_element_type=F32)
             + jnp.dot(a_qk, v_new, preferred_element_type=F32))
        g_last = gc_col[C - 1:C, :]
        kd = kh * jnp.exp(g_last - gc_col)
        s_ref[h] = s * jnp.exp(g_last) + _dot_tn(kd, v_new)
        o = o * lax.rsqrt(jnp.mean(o * o, axis=-1, keepdims=True) + NORM_EPS) * og_ref[...]
        o_ref[0, :, sl] = o * _silu(z_ref[0, :, sl])


def _gdn_scan(q, k, v, sg, gl, z, out_gain, B, T):
    C = GDN_CHUNK
    tok = lambda c: pl.BlockSpec((1, C, c), lambda b, i: (b, i, 0))
    return pl.pallas_call(
        _gdn_scan_kernel,
        grid=(B, T // C),
        in_specs=[tok(GDN_WIDTH), tok(GDN_WIDTH), tok(GDN_WIDTH), tok(SMALL_WIDTH), tok(SMALL_WIDTH),
                  tok(GDN_WIDTH), _resident((1, GDN_HEAD_DIM))],
        out_specs=tok(GDN_WIDTH),
        out_shape=jax.ShapeDtypeStruct((B, T, GDN_WIDTH), F32),
        scratch_shapes=[pltpu.VMEM((GDN_HEADS, GDN_HEAD_DIM, GDN_HEAD_DIM), F32)],
        compiler_params=_cparams("parallel", "arbitrary"),
        name="gdn_scan",
    )(q, k, v, sg, gl, z.reshape(B, T, GDN_WIDTH), out_gain.reshape(1, GDN_HEAD_DIM))


def _out_proj_kernel(x_ref, a_ref, b_ref, w_ref, o_ref):
    acc = jnp.dot(a_ref[...].astype(BF16), w_ref[0, :NSA_WIDTH, :], preferred_element_type=F32)
    acc = acc + jnp.dot(b_ref[...].astype(BF16), w_ref[0, NSA_WIDTH:, :], preferred_element_type=F32)
    o_ref[...] = x_ref[...] + acc


def _out_proj(x2, o_nsa, o_gdn, w_all, layer, tm=512):
    m = x2.shape[0]
    tm = min(tm, m)
    row = lambda c: pl.BlockSpec((tm, c), lambda i: (i, 0))
    return pl.pallas_call(
        _out_proj_kernel,
        grid=(m // tm,),
        in_specs=[row(D_MODEL), row(NSA_WIDTH), row(GDN_WIDTH),
                  pl.BlockSpec((1, NSA_WIDTH + GDN_WIDTH, D_MODEL), lambda i: (layer, 0, 0),
                               pipeline_mode=pl.Buffered(1))],
        out_specs=row(D_MODEL),
        out_shape=jax.ShapeDtypeStruct((m, D_MODEL), F32),
        compiler_params=_cparams("parallel"),
        name="out_proj",
    )(x2, o_nsa.reshape(m, NSA_WIDTH), o_gdn.reshape(m, GDN_WIDTH), w_all)


FFN_CHUNK = 256


def _ffn_kernel(x_ref, g_ref, wu_ref, cw_ref, cb_ref, wd_ref, o_ref, hist_ref):
    @pl.when(pl.program_id(1) == 0)
    def _():
        hist_ref[...] = jnp.zeros_like(hist_ref)

    x = x_ref[0]
    tm = x.shape[0]
    h = (x * lax.rsqrt(jnp.mean(x * x, axis=-1, keepdims=True) + NORM_EPS) * g_ref[...]).astype(BF16)
    acc = x

    def conv_cols(c0):
        u = jnp.dot(h, wu_ref[0, :, c0:c0 + FFN_CHUNK], preferred_element_type=F32)
        hist = hist_ref[:, c0:c0 + FFN_CHUNK]
        y = u * cw_ref[0, FFN_CONV - 1:FFN_CONV, c0:c0 + FFN_CHUNK]
        for k in range(1, FFN_CONV):
            y = y + _shift_rows(u, hist, k) * cw_ref[0, FFN_CONV - 1 - k:FFN_CONV - k, c0:c0 + FFN_CHUNK]
        hist_ref[:, c0:c0 + FFN_CHUNK] = u[tm - SUBLANES:, :]
        return y + cb_ref[0, :, c0:c0 + FFN_CHUNK]

    for c in range(D_FF // FFN_CHUNK):
        gate = conv_cols(c * FFN_CHUNK)
        up = conv_cols(D_FF + c * FFN_CHUNK)
        act = (_silu(gate) * up).astype(BF16)
        acc = acc + jnp.dot(act, wd_ref[0, c * FFN_CHUNK:(c + 1) * FFN_CHUNK, :], preferred_element_type=F32)
    o_ref[0] = acc


def _ffn(x3, gain, wu_all, cw_all, cb_all, wd_all, layer, tm=512):
    B, T, _ = x3.shape
    tm = min(tm, T)
    lay = lambda s1, s2: pl.BlockSpec((1, s1, s2), lambda b, i: (layer, 0, 0), pipeline_mode=pl.Buffered(1))
    return pl.pallas_call(
        _ffn_kernel,
        grid=(B, T // tm),
        in_specs=[pl.BlockSpec((1, tm, D_MODEL), lambda b, i: (b, i, 0)),
                  _resident((1, D_MODEL)),
                  lay(D_MODEL, 2 * D_FF), lay(SUBLANES, 2 * D_FF), lay(1, 2 * D_FF), lay(D_FF, D_MODEL)],
        out_specs=pl.BlockSpec((1, tm, D_MODEL), lambda b, i: (b, i, 0)),
        out_shape=jax.ShapeDtypeStruct((B, T, D_MODEL), F32),
        scratch_shapes=[pltpu.VMEM((SUBLANES, 2 * D_FF), F32)],
        compiler_params=_cparams("parallel", "arbitrary"),
        name="ffn",
    )(x3, gain.reshape(1, D_MODEL), wu_all, cw_all, cb_all, wd_all)


def kernel(x, positions, attn_norm, w_in, nsa_q_norm, nsa_k_norm, cmp_pe, cmp_w1, cmp_w2, nsa_out_norm,
           gdn_conv_w, gdn_a_log, gdn_dt_bias, gdn_out_norm, w_out, ffn_norm, w_up, ffn_conv_w,
           ffn_conv_b, w_down):
    B, T, _ = x.shape
    depth = w_in.shape[0]
    M = B * T
    tabs = _rope_tables(positions)
    w_in_r = _regroup_w_in(w_in)
    pe, w1g, w2g = _compress_weights(cmp_pe, cmp_w1, cmp_w2)
    w_out_b = w_out.astype(BF16)
    w_up_b = w_up.astype(BF16)
    w_down_b = w_down.astype(BF16)
    cw = jnp.concatenate([ffn_conv_w, jnp.zeros((depth, SUBLANES - FFN_CONV, 2 * D_FF), F32)], axis=1)
    cb = ffn_conv_b.reshape(depth, 1, 2 * D_FF)

    for l in range(depth):
        x2 = x.reshape(M, D_MODEL)
        q_raw, kv_raw, gqkv, z, small = _in_proj(x2, attn_norm[l], w_in_r, l)
        q, kc_in, kst, vs, kwt, vw = _nsa_prep(q_raw, kv_raw, tabs, nsa_q_norm[l], nsa_k_norm[l], B, T)
        kct_all, vc_all = _compress(kc_in, kv_raw.reshape(B, T, NSA_KV_WIDTH), pe, w1g, w2g,
                                    nsa_k_norm[l, 0], l, B, T)
        gq, gk, gv, sg, gl = _gdn_prep(gqkv, small, gdn_conv_w[l], gdn_a_log[l], gdn_dt_bias[l], B, T)
        o_nsa = _nsa_attn(q, sg, kct_all[:B], vc_all[B:], kst, vs, kwt, vw, nsa_out_norm[l], B, T)
        o_gdn = _gdn_scan(gq, gk, gv, sg, gl, z, gdn_out_norm[l], B, T)
        x2 = _out_proj(x2, o_nsa, o_gdn, w_out_b, l)
        x = _ffn(x2.reshape(B, T, D_MODEL), ffn_norm[l], w_up_b, cw, cb, w_down_b, l)
    return x
```

```python
import functools

import numpy as np
import jax
import jax.numpy as jnp
from jax import lax
from jax.experimental import pallas as pl
from jax.experimental.pallas import tpu as pltpu

D_MODEL = 1024
NSA_HEADS = 8
NSA_KV_GROUPS = 2
NSA_HPG = NSA_HEADS // NSA_KV_GROUPS
NSA_HEAD_DIM = 64
N_BRANCH = 3
CMP_BLOCK = 32
CMP_STRIDE = 16
CMP_HIDDEN = 256
SLC_BLOCK = 64
SLC_TOP_N = 16
WINDOW = 512
ROPE_THETA = 500000.0
ROPE_DIM = NSA_HEAD_DIM // 4
GDN_HEADS = 4
GDN_HEAD_DIM = 128
GDN_CONV = 4
GDN_CHUNK = 64
NSA_WIDTH = NSA_HEADS * NSA_HEAD_DIM
GDN_WIDTH = GDN_HEADS * GDN_HEAD_DIM
NSA_KV_WIDTH = N_BRANCH * 2 * NSA_KV_GROUPS * NSA_HEAD_DIM
NSA_GATE_WIDTH = NSA_HEADS * N_BRANCH
D_FF = 2816
FFN_CONV = 3
NORM_EPS = 1e-6

LANES = 128
SUBLANES = 8
SMALL_WIDTH = LANES
GDN_A_COL = NSA_GATE_WIDTH
GDN_B_COL = NSA_GATE_WIDTH + GDN_HEADS
VMEM_LIMIT = 56 * 1024 * 1024
NEG = -1e30
LOG2E = 1.4426950408889634

F32 = jnp.float32
BF16 = jnp.bfloat16


def _cparams(*sem):
    return pltpu.CompilerParams(dimension_semantics=sem, vmem_limit_bytes=VMEM_LIMIT)


def _resident(shape):
    nd = len(shape)
    return pl.BlockSpec(shape, lambda *_: (0,) * nd, pipeline_mode=pl.Buffered(1))


def _silu(x):
    return x * (1.0 / (1.0 + jnp.exp(-x)))


def _sigmoid(x):
    return 1.0 / (1.0 + jnp.exp(-x))


def _rope_kernel(pos_ref, inv_ref, c_ref, s1_ref, s2_ref):
    ang = pos_ref[...].astype(F32) * inv_ref[...]
    lane = lax.broadcasted_iota(jnp.int32, ang.shape, 1) % NSA_HEAD_DIM
    half = ROPE_DIM // 2
    cos = jnp.cos(ang)
    sin = jnp.sin(ang)
    c_ref[...] = jnp.where(lane < ROPE_DIM, cos, 1.0)
    s1_ref[...] = jnp.where((lane >= half) & (lane < ROPE_DIM), sin, 0.0)
    s2_ref[...] = jnp.where(lane < half, -sin, 0.0)


def _rope_tables(positions):
    m = positions.size
    tm = min(m, 1024)
    inv = ROPE_THETA ** (-jnp.arange(0, ROPE_DIM, 2, dtype=F32) / ROPE_DIM)
    lane = np.arange(LANES) % NSA_HEAD_DIM
    inv_row = jnp.where(lane < ROPE_DIM, inv[lane % (ROPE_DIM // 2)], 0.0).reshape(1, LANES)
    tab = jax.ShapeDtypeStruct((m, LANES), F32)
    return pl.pallas_call(
        _rope_kernel,
        grid=(m // tm,),
        in_specs=[pl.BlockSpec((tm, 1), lambda i: (i, 0)), _resident((1, LANES))],
        out_specs=[pl.BlockSpec((tm, LANES), lambda i: (i, 0))] * 3,
        out_shape=[tab] * 3,
        compiler_params=_cparams("parallel"),
        name="rope_tables",
    )(positions.reshape(m, 1), inv_row)


IN_SEG = (NSA_WIDTH, NSA_KV_WIDTH, 3 * GDN_WIDTH, GDN_WIDTH, SMALL_WIDTH)


def _in_proj_kernel(x_ref, g_ref, w_ref, q_ref, kv_ref, gq_ref, z_ref, sm_ref):
    x = x_ref[...]
    h = x * lax.rsqrt(jnp.mean(x * x, axis=-1, keepdims=True) + NORM_EPS) * g_ref[...]
    h = h.astype(BF16)
    off = 0
    for ref, width in zip((q_ref, kv_ref, gq_ref, z_ref, sm_ref), IN_SEG):
        ref[...] = jnp.dot(h, w_ref[0, :, off:off + width], preferred_element_type=F32)
        off += width


def _regroup_w_in(w_in):
    o_q = 0
    o_kv = o_q + NSA_WIDTH
    o_g = o_kv + NSA_KV_WIDTH
    o_gq = o_g + NSA_GATE_WIDTH
    o_a = o_gq + 3 * GDN_WIDTH
    o_b = o_a + GDN_HEADS
    o_z = o_b + GDN_HEADS
    pad = SMALL_WIDTH - NSA_GATE_WIDTH - 2 * GDN_HEADS
    parts = [w_in[..., o_q:o_g], w_in[..., o_gq:o_a], w_in[..., o_z:o_z + GDN_WIDTH],
             w_in[..., o_g:o_gq], w_in[..., o_a:o_z],
             jnp.zeros(w_in.shape[:-1] + (pad,), w_in.dtype)]
    return jnp.concatenate(parts, axis=-1).astype(BF16)


def _in_proj(x2, gain, w_all, layer, tm=512):
    m = x2.shape[0]
    tm = min(tm, m)
    npad = sum(IN_SEG)
    outs = [jax.ShapeDtypeStruct((m, w), F32) for w in IN_SEG]
    return pl.pallas_call(
        _in_proj_kernel,
        grid=(m // tm,),
        in_specs=[pl.BlockSpec((tm, D_MODEL), lambda i: (i, 0)),
                  _resident((1, D_MODEL)),
                  pl.BlockSpec((1, D_MODEL, npad), lambda i: (layer, 0, 0), pipeline_mode=pl.Buffered(1))],
        out_specs=[pl.BlockSpec((tm, w), lambda i: (i, 0)) for w in IN_SEG],
        out_shape=outs,
        compiler_params=_cparams("parallel"),
        name="in_proj",
    )(x2, gain.reshape(1, D_MODEL), w_all)


def _head_rms(x, gain_row):
    lane = lax.broadcasted_iota(jnp.int32, x.shape, 1)
    x2 = x * x
    left = jnp.sum(jnp.where(lane < NSA_HEAD_DIM, x2, 0.0), axis=-1, keepdims=True)
    right = jnp.sum(jnp.where(lane >= NSA_HEAD_DIM, x2, 0.0), axis=-1, keepdims=True)
    ms = jnp.where(lane < NSA_HEAD_DIM, left, right) * (1.0 / NSA_HEAD_DIM)
    return x * lax.rsqrt(ms + NORM_EPS) * gain_row


def _rope(x, c, s1, s2):
    half = ROPE_DIM // 2
    return x * c + pltpu.roll(x, half, 1) * s1 + pltpu.roll(x, LANES - half, 1) * s2


V_ROWS = NSA_HEAD_DIM + 16


def _values_with_ones(v):
    vt = v.T
    tm = v.shape[0]
    extra = jnp.where(lax.broadcasted_iota(jnp.int32, (V_ROWS - NSA_HEAD_DIM, tm), 0) == 0, 1.0, 0.0)
    parts = []
    for g in range(NSA_KV_GROUPS):
        parts += [vt[g * NSA_HEAD_DIM:(g + 1) * NSA_HEAD_DIM], extra]
    return jnp.concatenate(parts, axis=0).astype(BF16)


def _nsa_prep_kernel(q_ref, kv_ref, c_ref, s1_ref, s2_ref, qg_ref, kg_ref,
                     qt_ref, kc_ref, ks_ref, vst_ref, kw_ref, vwt_ref):
    c, s1, s2 = c_ref[...], s1_ref[...], s2_ref[...]
    qg = qg_ref[...]
    tm = c.shape[0]
    for j in range(NSA_WIDTH // LANES):
        x = q_ref[0, :, j * LANES:(j + 1) * LANES]
        y = _rope(_head_rms(x, qg), c, s1, s2) * (NSA_HEAD_DIM ** -0.5) * LOG2E
        qt_ref[0, j * LANES:(j + 1) * LANES, :] = y.T.astype(BF16)
    kv = lambda i: kv_ref[0, :, i * LANES:(i + 1) * LANES]
    kc_ref[0] = _rope(kv(0), c, s1, s2)
    ks = _rope(_head_rms(kv(2), kg_ref[1:2, :]), c, s1, s2)
    lane = lax.broadcasted_iota(jnp.int32, (tm, LANES), 1)
    blk = (pl.program_id(1) * tm + lax.broadcasted_iota(jnp.int32, (tm, LANES), 0)) // SLC_BLOCK
    onehot = jnp.where(lane - NSA_HEAD_DIM == blk, 1.0, 0.0)
    ks_ref[0, :, :LANES] = jnp.where(lane < NSA_HEAD_DIM, ks, onehot).astype(BF16)
    ks_ref[0, :, LANES:] = jnp.where(lane < NSA_HEAD_DIM, pltpu.roll(ks, NSA_HEAD_DIM, 1), onehot).astype(BF16)
    vst_ref[0] = _values_with_ones(kv(3))
    kw_ref[0] = _rope(_head_rms(kv(4), kg_ref[2:3, :]), c, s1, s2).astype(BF16)
    vwt_ref[0] = _values_with_ones(kv(5))


def _nsa_prep(q_raw, kv_raw, tabs, q_gain, k_gain, B, T, tm=512):
    tm = min(tm, T)
    nt = T // tm
    c, s1, s2 = tabs
    qg = jnp.tile(q_gain.reshape(1, NSA_HEAD_DIM), (1, 2))
    kg = jnp.tile(k_gain, (1, 2))
    kg = jnp.concatenate([kg, jnp.zeros((5, LANES), F32)], axis=0)
    tok = lambda w: pl.BlockSpec((1, tm, w), lambda b, i: (b, i, 0))
    tab = pl.BlockSpec((tm, LANES), lambda b, i: (b * nt + i, 0))
    tr = pl.BlockSpec((1, NSA_KV_GROUPS * V_ROWS, tm), lambda b, i: (b, 0, i))
    return pl.pallas_call(
        _nsa_prep_kernel,
        grid=(B, nt),
        in_specs=[tok(NSA_WIDTH), tok(NSA_KV_WIDTH), tab, tab, tab,
                  _resident((1, LANES)), _resident((8, LANES))],
        out_specs=[pl.BlockSpec((1, NSA_WIDTH, tm), lambda b, i: (b, 0, i)),
                   tok(LANES), tok(2 * LANES), tr, tok(LANES), tr],
        out_shape=[jax.ShapeDtypeStruct((B, NSA_WIDTH, T), BF16),
                   jax.ShapeDtypeStruct((B, T, LANES), F32),
                   jax.ShapeDtypeStruct((B, T, 2 * LANES), BF16),
                   jax.ShapeDtypeStruct((B, NSA_KV_GROUPS * V_ROWS, T), BF16),
                   jax.ShapeDtypeStruct((B, T, LANES), BF16),
                   jax.ShapeDtypeStruct((B, NSA_KV_GROUPS * V_ROWS, T), BF16)],
        compiler_params=_cparams("parallel", "parallel"),
        name="nsa_prep",
    )(q_raw.reshape(B, T, NSA_WIDTH), kv_raw.reshape(B, T, NSA_KV_WIDTH), c, s1, s2, qg, kg)


CMP_ROW = CMP_STRIDE * LANES
CMP_HID2 = NSA_KV_GROUPS * CMP_HIDDEN


def _compress_kernel(r_ref, pe_ref, w1_ref, w2_ref, kg_ref, t_ref, n_ref):
    is_k = pl.program_id(0) == 0
    r = r_ref[0, 0]
    ha = jnp.dot((r + pe_ref[0, 0:1, :]).astype(BF16), w1_ref[0, 0], preferred_element_type=F32)
    hb = jnp.dot((r + pe_ref[0, 1:2, :]).astype(BF16), w1_ref[0, 1], preferred_element_type=F32)
    n16 = r.shape[0]
    hid = ha + pltpu.roll(hb, n16 - 1, 0)
    y = jnp.dot(_silu(hid).astype(BF16), w2_ref[0], preferred_element_type=F32)
    y = jnp.where(is_k, _head_rms(y, kg_ref[...]), y)
    t_ref[0] = y.T.astype(BF16)
    n_ref[0] = y.astype(BF16)


def _compress_weights(cmp_pe, cmp_w1, cmp_w2):
    L = cmp_w1.shape[0]
    G, DH, H = NSA_KV_GROUPS, NSA_HEAD_DIM, CMP_HIDDEN
    w1 = cmp_w1.reshape(L, 2, 2, CMP_STRIDE, DH, H)
    eye = jnp.eye(G, dtype=cmp_w1.dtype)
    w1g = jnp.einsum('lkspdh,gf->lkspgdfh', w1, eye)
    w1g = w1g.reshape(L, 2, 2, CMP_ROW, G * H).astype(BF16)
    w2g = jnp.einsum('lkhd,gf->lkghfd', cmp_w2, eye).reshape(L, 2, G * H, G * DH).astype(BF16)
    pe = cmp_pe.reshape(L, 2, 2, CMP_STRIDE, 1, DH)
    pe = jnp.broadcast_to(pe, (L, 2, 2, CMP_STRIDE, G, DH)).reshape(L, 2, 2, CMP_ROW)
    return pe, w1g, w2g


def _compress(kc_in, kv_raw3, pe, w1g, w2g, k_gain0, layer, B, T):
    n16 = T // CMP_STRIDE
    rk = kc_in.reshape(B, n16, CMP_ROW)
    rv = kv_raw3[:, :, LANES:2 * LANES].reshape(B, n16, CMP_ROW)
    r = jnp.stack([rk, rv], axis=0)
    kg = jnp.tile(k_gain0.reshape(1, NSA_HEAD_DIM), (1, 2))
    return pl.pallas_call(
        _compress_kernel,
        grid=(2, B),
        in_specs=[pl.BlockSpec((1, 1, n16, CMP_ROW), lambda s, b: (s, b, 0, 0)),
                  pl.BlockSpec((1, 2, CMP_ROW), lambda s, b: (layer * 2 + s, 0, 0)),
                  pl.BlockSpec((1, 2, CMP_ROW, CMP_HID2), lambda s, b: (layer * 2 + s, 0, 0, 0)),
                  pl.BlockSpec((1, CMP_HID2, LANES), lambda s, b: (layer * 2 + s, 0, 0)),
                  _resident((1, LANES))],
        out_specs=[pl.BlockSpec((1, LANES, n16), lambda s, b: (s * B + b, 0, 0)),
                   pl.BlockSpec((1, n16, LANES), lambda s, b: (s * B + b, 0, 0))],
        out_shape=[jax.ShapeDtypeStruct((2 * B, LANES, n16), BF16),
                   jax.ShapeDtypeStruct((2 * B, n16, LANES), BF16)],
        compiler_params=_cparams("arbitrary", "parallel"),
        name="compress",
    )(r, pe.reshape(-1, 2, CMP_ROW), w1g.reshape(-1, 2, CMP_ROW, CMP_HID2),
      w2g.reshape(-1, CMP_HID2, LANES), kg)


ATT_TQ = 128
ATT_TK = 512
WIN_KEYS = WINDOW + ATT_TQ
SLC_MASK_DIM = 64


def _softmax_cols(s):
    m = jnp.max(s, axis=0, keepdims=True)
    p = jnp.where(m > 0.5 * NEG, jnp.exp2(s - m), 0.0)
    return p, 1.0 / jnp.maximum(jnp.sum(p, axis=0, keepdims=True), 1e-30)


def _flash_update(s, m, acc, v_ones):
    m_new = jnp.maximum(m, jnp.max(s, axis=0, keepdims=True))
    p = jnp.exp2((s - m_new).astype(BF16))
    acc = acc * jnp.exp2(m - m_new) + jnp.dot(v_ones, p, preferred_element_type=F32)
    return m_new, acc


def _flash_finish(acc):
    dh = NSA_HEAD_DIM
    return acc[:dh] * (1.0 / jnp.maximum(acc[dh:dh + 1], 1e-30))


def _nsa_attn_kernel(qt_ref, gt_ref, kc_ref, vct_ref, ks_ref, vst_ref, kw_ref, vwt_ref,
                     ovt_ref, ogt_ref, o_ref, *, n_cmp_pad):
    TQ, TK, H, DH, NB = ATT_TQ, ATT_TK, NSA_HPG, NSA_HEAD_DIM, SLC_MASK_DIM
    t0 = pl.program_id(1) * TQ
    heads = lambda a: jnp.concatenate([a] * H, axis=1)
    t_row = t0 + lax.broadcasted_iota(jnp.int32, (1, TQ), 1)

    ci = lax.broadcasted_iota(jnp.int32, (n_cmp_pad, TQ), 0)
    bias_c = heads(jnp.where(ci * CMP_STRIDE + (CMP_BLOCK - 1) <= t_row, 0.0, NEG))
    w0 = pl.multiple_of(jnp.maximum(t0 - WINDOW, 0), LANES)
    rel = t_row - (w0 + lax.broadcasted_iota(jnp.int32, (WIN_KEYS, TQ), 0))
    bias_w = heads(jnp.where((rel >= 0) & (rel < WINDOW), 0.0, NEG))
    kd = t0 // TK
    k0d = pl.multiple_of(kd * TK, TK)
    bias_d = heads(jnp.where(k0d + lax.broadcasted_iota(jnp.int32, (TK, TQ), 0) <= t_row, 0.0, NEG))

    jj = lax.broadcasted_iota(jnp.int32, (NB, TQ), 0)
    cur = jnp.broadcast_to(t_row // SLC_BLOCK, (NB, TQ))
    forced = (jj == 0) | (jj == cur) | (jj == cur - 1)
    j8 = lax.broadcasted_iota(jnp.int32, (SUBLANES, TQ), 0)

    sg_t = gt_ref[0].T
    zeros_q = jnp.zeros((DH, TQ), BF16)
    G = NSA_KV_GROUPS
    rhs_q, rhs_s, o_c = [], [], []

    for g in range(G):
        lo, hi = g * DH, (g + 1) * DH
        q_t = [qt_ref[0, (g * H + h) * DH:(g * H + h + 1) * DH, :] for h in range(H)]
        pad = (lambda x: jnp.concatenate([x, zeros_q], axis=0)) if g == 0 else \
              (lambda x: jnp.concatenate([zeros_q, x], axis=0))
        rhs_q.append(jnp.concatenate([pad(x) for x in q_t], axis=1))

        pc, inv_c = _softmax_cols(jnp.dot(kc_ref[0], rhs_q[g], preferred_element_type=F32) + bias_c)
        o_c.append(jnp.dot(vct_ref[0, lo:hi, :], pc.astype(BF16), preferred_element_type=F32) * inv_c)

        psum = pc[:, :TQ] * inv_c[:, :TQ]
        for h in range(1, H):
            psum = psum + pc[:, h * TQ:(h + 1) * TQ] * inv_c[:, h * TQ:(h + 1) * TQ]
        imp = jnp.dot(ovt_ref[...], psum, preferred_element_type=F32, precision=lax.Precision.HIGHEST)
        val = jnp.where(forced, jnp.inf, jnp.where(jj > cur, -jnp.inf, imp))
        vals = [val[SUBLANES * v:SUBLANES * (v + 1)] for v in range(NB // SUBLANES)]
        cnts = [jnp.zeros((SUBLANES, TQ), F32) for _ in vals]
        for jp in range(NB):
            row = val[jp:jp + 1, :]
            for v in range(len(vals)):
                if SUBLANES * v > jp:
                    beat = jnp.where(row >= vals[v], 1.0, 0.0)
                elif SUBLANES * v + SUBLANES - 1 < jp:
                    beat = jnp.where(row > vals[v], 1.0, 0.0)
                else:
                    beat = jnp.where(j8 > jp - SUBLANES * v, jnp.where(row >= vals[v], 1.0, 0.0),
                                     jnp.where(row > vals[v], 1.0, 0.0))
                cnts[v] = cnts[v] + beat
        cnt = jnp.concatenate(cnts, axis=0)
        mask_t = jnp.where((cnt < float(SLC_TOP_N)) & (jj <= cur), 0.0, NEG).astype(BF16)

        rhs_s.append(jnp.concatenate([jnp.concatenate([x, mask_t], axis=0) for x in q_t], axis=1))

    def slc_tile(k0, carry, bias):
        out = []
        for g in range(G):
            s = jnp.dot(ks_ref[0, pl.ds(k0, TK), g * LANES:(g + 1) * LANES], rhs_s[g], preferred_element_type=F32)
            if bias is not None:
                s = s + bias
            out.append(_flash_update(s, *carry[g], vst_ref[0, g * V_ROWS:(g + 1) * V_ROWS, pl.ds(k0, TK)]))
        return tuple(out)

    flash_init = (jnp.full((1, H * TQ), NEG, F32), jnp.zeros((V_ROWS, H * TQ), F32))
    carry = lax.fori_loop(0, kd, lambda kt, c: slc_tile(pl.multiple_of(kt * TK, TK), c, None), (flash_init,) * G)
    carry = slc_tile(k0d, carry, bias_d)

    out_rows = []
    for g in range(G):
        o_s = _flash_finish(carry[g][1])

        sw = jnp.dot(kw_ref[0, pl.ds(w0, WIN_KEYS), :], rhs_q[g], preferred_element_type=F32) + bias_w
        o_w = _flash_finish(_flash_update(sw, *flash_init, vwt_ref[0, g * V_ROWS:(g + 1) * V_ROWS,
                                                                   pl.ds(w0, WIN_KEYS)])[1])

        def gate(br):
            rows = [(g * H + h) * N_BRANCH + br for h in range(H)]
            return jnp.concatenate([sg_t[r:r + 1, :] for r in rows], axis=1)

        o = gate(0) * o_c[g] + gate(1) * o_s + gate(2) * o_w
        o = o * lax.rsqrt(jnp.mean(o * o, axis=0, keepdims=True) + NORM_EPS)
        for h in range(H):
            out_rows.append(o[:, h * TQ:(h + 1) * TQ] * ogt_ref[:, g * H + h:g * H + h + 1])

    o_ref[0] = jnp.concatenate(out_rows, axis=0).T


def _nsa_consts(T):
    n_cmp = (T - CMP_BLOCK) // CMP_STRIDE + 1
    n_slc = T // SLC_BLOCK
    n_cmp_pad = T // CMP_STRIDE
    cs = np.arange(n_cmp_pad) * CMP_STRIDE
    ss = np.arange(SLC_MASK_DIM) * SLC_BLOCK
    ov = ((cs[None, :] < ss[:, None] + SLC_BLOCK) & (cs[None, :] + CMP_BLOCK > ss[:, None]))
    ov = ov & (np.arange(n_cmp_pad)[None, :] < n_cmp) & (np.arange(SLC_MASK_DIM)[:, None] < n_slc)
    return jnp.asarray(ov, F32), n_cmp_pad


def _nsa_attn(qt, gates, kc, vct, ks, vst, kw, vwt, out_gain, B, T):
    ovt, n_cmp_pad = _nsa_consts(T)
    assert T // SLC_BLOCK <= SLC_MASK_DIM and T % ATT_TK == 0 and T >= WIN_KEYS
    nq = T // ATT_TQ
    ogt = jnp.zeros((NSA_HEAD_DIM, LANES), F32).at[:, :NSA_HEADS].set(out_gain.T)
    per_b = lambda s1, s2: pl.BlockSpec((1, s1, s2), lambda b, i: (b, 0, 0))
    return pl.pallas_call(
        functools.partial(_nsa_attn_kernel, n_cmp_pad=n_cmp_pad),
        grid=(B, nq),
        in_specs=[pl.BlockSpec((1, NSA_WIDTH, ATT_TQ), lambda b, i: (b, 0, i)),
                  pl.BlockSpec((1, ATT_TQ, SMALL_WIDTH), lambda b, i: (b, i, 0)),
                  per_b(n_cmp_pad, LANES), per_b(LANES, n_cmp_pad),
                  per_b(T, 2 * LANES), per_b(NSA_KV_GROUPS * V_ROWS, T),
                  per_b(T, LANES), per_b(NSA_KV_GROUPS * V_ROWS, T),
                  _resident((SLC_MASK_DIM, n_cmp_pad)), _resident((NSA_HEAD_DIM, LANES))],
        out_specs=pl.BlockSpec((1, ATT_TQ, NSA_WIDTH), lambda b, i: (b, i, 0)),
        out_shape=jax.ShapeDtypeStruct((B, T, NSA_WIDTH), F32),
        compiler_params=_cparams("parallel", "parallel"),
        name="nsa_attn",
    )(qt, gates, kc, vct, ks, vst, kw, vwt, ovt, ogt)


def _shift_rows(x, hist, k):
    xr = pltpu.roll(x, k, 0)
    hr = pltpu.roll(hist, k, 0)
    row = lax.broadcasted_iota(jnp.int32, hist.shape, 0)
    top = jnp.where(row < k, hr, xr[:SUBLANES])
    return jnp.concatenate([top, xr[SUBLANES:]], axis=0)


def _causal_conv(x, hist, w_ref, width):
    y = x * w_ref[width - 1:width, :]
    for k in range(1, width):
        y = y + _shift_rows(x, hist, k) * w_ref[width - 1 - k:width - k, :]
    return y


def _gdn_prep_kernel(x_ref, sm_ref, w_ref, alog_ref, dtb_ref,
                     q_ref, k_ref, v_ref, sg_ref, gl_ref, hist_ref):
    @pl.when(pl.program_id(1) == 0)
    def _():
        hist_ref[...] = jnp.zeros_like(hist_ref)

    x = x_ref[0]
    y = _silu(_causal_conv(x, hist_ref[...], w_ref, GDN_CONV))
    hist_ref[...] = x[x.shape[0] - SUBLANES:, :]
    for h in range(GDN_HEADS):
        for src, dst, scale in ((0, q_ref, GDN_HEAD_DIM ** -0.5), (1, k_ref, None)):
            c0 = src * GDN_WIDTH + h * GDN_HEAD_DIM
            t = y[:, c0:c0 + GDN_HEAD_DIM]
            t = t * lax.rsqrt(jnp.sum(t * t, axis=-1, keepdims=True) + NORM_EPS)
            if scale is not None:
                t = t * scale
            dst[0, :, h * GDN_HEAD_DIM:(h + 1) * GDN_HEAD_DIM] = t
    v_ref[0] = y[:, 2 * GDN_WIDTH:]
    sm = sm_ref[0]
    sg_ref[0] = _sigmoid(sm)
    a = sm + dtb_ref[...]
    softplus = jnp.maximum(a, 0.0) + jnp.log(1.0 + jnp.exp(-jnp.abs(a)))
    gl_ref[0] = -jnp.exp(alog_ref[...]) * softplus


def _gdn_prep(gqkv, small, conv_w, a_log, dt_bias, B, T, tm=256):
    tm = min(tm, T)
    w = jnp.concatenate([conv_w, jnp.zeros((SUBLANES - GDN_CONV, 3 * GDN_WIDTH), F32)], axis=0)
    place = lambda v: jnp.zeros((1, SMALL_WIDTH), F32).at[0, GDN_A_COL:GDN_A_COL + GDN_HEADS].set(v)
    tok = lambda c: pl.BlockSpec((1, tm, c), lambda b, i: (b, i, 0))
    return pl.pallas_call(
        _gdn_prep_kernel,
        grid=(B, T // tm),
        in_specs=[tok(3 * GDN_WIDTH), tok(SMALL_WIDTH), _resident((SUBLANES, 3 * GDN_WIDTH)),
                  _resident((1, SMALL_WIDTH)), _resident((1, SMALL_WIDTH))],
        out_specs=[tok(GDN_WIDTH), tok(GDN_WIDTH), tok(GDN_WIDTH), tok(SMALL_WIDTH), tok(SMALL_WIDTH)],
        out_shape=[jax.ShapeDtypeStruct((B, T, GDN_WIDTH), F32)] * 3
                  + [jax.ShapeDtypeStruct((B, T, SMALL_WIDTH), F32)] * 2,
        scratch_shapes=[pltpu.VMEM((SUBLANES, 3 * GDN_WIDTH), F32)],
        compiler_params=_cparams("parallel", "arbitrary"),
        name="gdn_prep",
    )(gqkv.reshape(B, T, 3 * GDN_WIDTH), small.reshape(B, T, SMALL_WIDTH), w, place(a_log), place(dt_bias))


GDN_NC = 8


def _bmm(a, b):
    return jnp.einsum('cij,cjk->cik', a.astype(BF16), b.astype(BF16), preferred_element_type=F32)


def _bmm_nt(a, b):
    return jnp.einsum('cid,cjd->cij', a.astype(BF16), b.astype(BF16), preferred_element_type=F32)


def _unit_lower_inverse(a, ii, jj):
    eye = (ii == jj).astype(F32)
    blk = lambda n: (ii // n) == (jj // n)
    ad = jnp.where(blk(16), a, 0.0)
    t = eye - ad
    p = ad
    for _ in range(3):
        p = _bmm(p, p)
        t = _bmm(t, eye + p)
    for n in (32, 64):
        off = jnp.where(blk(n) & ~blk(n // 2), a, 0.0)
        t = t - _bmm(_bmm(t, off), t)
    return t


def _gdn_scan_kernel(q_ref, k_ref, v_ref, sg_ref, gl_ref, z_ref, og_ref, o_ref,
                     s_ref, u_scr, wq_scr, akd_scr, egl_scr, o_scr):
    C, DK, NC, H = GDN_CHUNK, GDN_HEAD_DIM, GDN_NC, GDN_HEADS

    @pl.when(pl.program_id(1) == 0)
    def _():
        s_ref[...] = jnp.zeros_like(s_ref)

    ii = lax.broadcasted_iota(jnp.int32, (NC, C, C), 1)
    jj = lax.broadcasted_iota(jnp.int32, (NC, C, C), 2)
    causal = ii >= jj
    sg = sg_ref[0]
    gl = gl_ref[0]

    for h in range(H):
        sl = slice(h * DK, (h + 1) * DK)
        q3 = q_ref[0, :, sl].reshape(NC, C, DK)
        k3 = k_ref[0, :, sl].reshape(NC, C, DK)
        v3 = v_ref[0, :, sl].reshape(NC, C, DK)
        g_col = gl[:, GDN_A_COL + h:GDN_A_COL + h + 1].reshape(NC, C, 1)
        b_col = sg[:, GDN_B_COL + h:GDN_B_COL + h + 1].reshape(NC, C, 1)
        g_mat = jnp.broadcast_to(g_col, (NC, C, C))
        g_row = jnp.sum(jnp.where(ii == jj, g_mat, 0.0), axis=1, keepdims=True)
        gc_col = jnp.sum(jnp.where(causal, jnp.broadcast_to(g_row, (NC, C, C)), 0.0), axis=2, keepdims=True)
        gc_row = jnp.sum(jnp.where(ii <= jj, g_mat, 0.0), axis=1, keepdims=True)
        decay = jnp.where(causal, jnp.exp(jnp.where(causal, gc_col - gc_row, 0.0)), 0.0)
        kb = k3 * b_col
        kq = _bmm_nt(jnp.concatenate([kb, q3], axis=1), k3)
        a_kk = jnp.where(ii > jj, kq[:, :C] * decay, 0.0)
        a_qk = kq[:, C:] * decay
        t_inv = _unit_lower_inverse(a_kk, ii, jj)
        egc = jnp.exp(gc_col)
        uw = _bmm(t_inv, jnp.concatenate([v3 * b_col, kb * egc], axis=2))
        g_last = gc_col[:, C - 1:C, :]
        kd = k3 * jnp.exp(g_last - gc_col)
        u_scr[h] = uw[:, :, :DK]
        wq_scr[h] = jnp.concatenate([uw[:, :, DK:], q3 * egc], axis=1).astype(BF16)
        akd_scr[h] = jnp.concatenate([a_qk, jnp.swapaxes(kd, 1, 2)], axis=1).astype(BF16)
        egl_scr[h] = jnp.broadcast_to(jnp.exp(g_last), (NC, SUBLANES, DK))

    def step(c, carry):
        r0 = pl.multiple_of(c * C, C)
        for h in range(H):
            s = s_ref[h]
            r = jnp.dot(wq_scr[h, c], s.astype(BF16), preferred_element_type=F32)
            v_new = (u_scr[h, c] - r[:C]).astype(BF16)
            r2 = jnp.dot(akd_scr[h, c], v_new, preferred_element_type=F32)
            o_scr[pl.ds(r0, C), h * DK:(h + 1) * DK] = r[C:] + r2[:C]
            s_ref[h] = s * egl_scr[h, c][0:1, :] + r2[C:]
        return carry

    lax.fori_loop(0, NC, step, 0)

    for h in range(H):
        sl = slice(h * DK, (h + 1) * DK)
        o = o_scr[:, sl]
        o = o * lax.rsqrt(jnp.mean(o * o, axis=-1, keepdims=True) + NORM_EPS) * og_ref[...]
        o_ref[0, :, sl] = o * _silu(z_ref[0, :, sl])


def _gdn_scan(q, k, v, sg, gl, z, out_gain, B, T):
    C, NC, H, DK = GDN_CHUNK, GDN_NC, GDN_HEADS, GDN_HEAD_DIM
    ts = NC * C
    tok = lambda c: pl.BlockSpec((1, ts, c), lambda b, i: (b, i, 0))
    return pl.pallas_call(
        _gdn_scan_kernel,
        grid=(B, T // ts),
        in_specs=[tok(GDN_WIDTH), tok(GDN_WIDTH), tok(GDN_WIDTH), tok(SMALL_WIDTH), tok(SMALL_WIDTH),
                  tok(GDN_WIDTH), _resident((1, GDN_HEAD_DIM))],
        out_specs=tok(GDN_WIDTH),
        out_shape=jax.ShapeDtypeStruct((B, T, GDN_WIDTH), F32),
        scratch_shapes=[pltpu.VMEM((H, DK, DK), F32),
                        pltpu.VMEM((H, NC, C, DK), F32),
                        pltpu.VMEM((H, NC, 2 * C, DK), BF16),
                        pltpu.VMEM((H, NC, C + DK, C), BF16),
                        pltpu.VMEM((H, NC, SUBLANES, DK), F32),
                        pltpu.VMEM((ts, GDN_WIDTH), F32)],
        compiler_params=_cparams("parallel", "arbitrary"),
        name="gdn_scan",
    )(q, k, v, sg, gl, z.reshape(B, T, GDN_WIDTH), out_gain.reshape(1, GDN_HEAD_DIM))


def _out_proj_kernel(x_ref, a_ref, b_ref, w_ref, o_ref):
    acc = jnp.dot(a_ref[...].astype(BF16), w_ref[0, :NSA_WIDTH, :], preferred_element_type=F32)
    acc = acc + jnp.dot(b_ref[...].astype(BF16), w_ref[0, NSA_WIDTH:, :], preferred_element_type=F32)
    o_ref[...] = x_ref[...] + acc


def _out_proj(x2, o_nsa, o_gdn, w_all, layer, tm=512):
    m = x2.shape[0]
    tm = min(tm, m)
    row = lambda c: pl.BlockSpec((tm, c), lambda i: (i, 0))
    return pl.pallas_call(
        _out_proj_kernel,
        grid=(m // tm,),
        in_specs=[row(D_MODEL), row(NSA_WIDTH), row(GDN_WIDTH),
                  pl.BlockSpec((1, NSA_WIDTH + GDN_WIDTH, D_MODEL), lambda i: (layer, 0, 0),
                               pipeline_mode=pl.Buffered(1))],
        out_specs=row(D_MODEL),
        out_shape=jax.ShapeDtypeStruct((m, D_MODEL), F32),
        compiler_params=_cparams("parallel"),
        name="out_proj",
    )(x2, o_nsa.reshape(m, NSA_WIDTH), o_gdn.reshape(m, GDN_WIDTH), w_all)


FFN_CHUNK = 256


def _ffn_kernel(x_ref, g_ref, wu_ref, cw_ref, cb_ref, wd_ref, o_ref, hist_ref):
    @pl.when(pl.program_id(1) == 0)
    def _():
        hist_ref[...] = jnp.zeros_like(hist_ref)

    x = x_ref[0]
    tm = x.shape[0]
    h = (x * lax.rsqrt(jnp.mean(x * x, axis=-1, keepdims=True) + NORM_EPS) * g_ref[...]).astype(BF16)
    acc = x

    def conv_cols(c0):
        u = jnp.dot(h, wu_ref[0, :, c0:c0 + FFN_CHUNK], preferred_element_type=F32)
        hist = hist_ref[:, c0:c0 + FFN_CHUNK]
        y = u * cw_ref[0, FFN_CONV - 1:FFN_CONV, c0:c0 + FFN_CHUNK]
        for k in range(1, FFN_CONV):
            y = y + _shift_rows(u, hist, k) * cw_ref[0, FFN_CONV - 1 - k:FFN_CONV - k, c0:c0 + FFN_CHUNK]
        hist_ref[:, c0:c0 + FFN_CHUNK] = u[tm - SUBLANES:, :]
        return y + cb_ref[0, :, c0:c0 + FFN_CHUNK]

    for c in range(D_FF // FFN_CHUNK):
        gate = conv_cols(c * FFN_CHUNK)
        up = conv_cols(D_FF + c * FFN_CHUNK)
        act = (_silu(gate) * up).astype(BF16)
        acc = acc + jnp.dot(act, wd_ref[0, c * FFN_CHUNK:(c + 1) * FFN_CHUNK, :], preferred_element_type=F32)
    o_ref[0] = acc


def _ffn(x3, gain, wu_all, cw_all, cb_all, wd_all, layer, tm=512):
    B, T, _ = x3.shape
    tm = min(tm, T)
    lay = lambda s1, s2: pl.BlockSpec((1, s1, s2), lambda b, i: (layer, 0, 0), pipeline_mode=pl.Buffered(1))
    return pl.pallas_call(
        _ffn_kernel,
        grid=(B, T // tm),
        in_specs=[pl.BlockSpec((1, tm, D_MODEL), lambda b, i: (b, i, 0)),
                  _resident((1, D_MODEL)),
                  lay(D_MODEL, 2 * D_FF), lay(SUBLANES, 2 * D_FF), lay(1, 2 * D_FF), lay(D_FF, D_MODEL)],
        out_specs=pl.BlockSpec((1, tm, D_MODEL), lambda b, i: (b, i, 0)),
        out_shape=jax.ShapeDtypeStruct((B, T, D_MODEL), F32),
        scratch_shapes=[pltpu.VMEM((SUBLANES, 2 * D_FF), F32)],
        compiler_params=_cparams("parallel", "arbitrary"),
        name="ffn",
    )(x3, gain.reshape(1, D_MODEL), wu_all, cw_all, cb_all, wd_all)


def kernel(x, positions, attn_norm, w_in, nsa_q_norm, nsa_k_norm, cmp_pe, cmp_w1, cmp_w2, nsa_out_norm,
           gdn_conv_w, gdn_a_log, gdn_dt_bias, gdn_out_norm, w_out, ffn_norm, w_up, ffn_conv_w,
           ffn_conv_b, w_down):
    B, T, _ = x.shape
    depth = w_in.shape[0]
    M = B * T
    tabs = _rope_tables(positions)
    w_in_r = _regroup_w_in(w_in)
    pe, w1g, w2g = _compress_weights(cmp_pe, cmp_w1, cmp_w2)
    w_out_b = w_out.astype(BF16)
    w_up_b = w_up.astype(BF16)
    w_down_b = w_down.astype(BF16)
    cw = jnp.concatenate([ffn_conv_w, jnp.zeros((depth, SUBLANES - FFN_CONV, 2 * D_FF), F32)], axis=1)
    cb = ffn_conv_b.reshape(depth, 1, 2 * D_FF)

    for l in range(depth):
        x2 = x.reshape(M, D_MODEL)
        q_raw, kv_raw, gqkv, z, small = _in_proj(x2, attn_norm[l], w_in_r, l)
        qt, kc_in, ks, vst, kw, vwt = _nsa_prep(q_raw, kv_raw, tabs, nsa_q_norm[l], nsa_k_norm[l], B, T)
        cmp_t, cmp_n = _compress(kc_in, kv_raw.reshape(B, T, NSA_KV_WIDTH), pe, w1g, w2g,
                                 nsa_k_norm[l, 0], l, B, T)
        gq, gk, gv, sg, gl = _gdn_prep(gqkv, small, gdn_conv_w[l], gdn_a_log[l], gdn_dt_bias[l], B, T)
        o_nsa = _nsa_attn(qt, sg, cmp_n[:B], cmp_t[B:], ks, vst, kw, vwt, nsa_out_norm[l], B, T)
        o_gdn = _gdn_scan(gq, gk, gv, sg, gl, z, gdn_out_norm[l], B, T)
        x2 = _out_proj(x2, o_nsa, o_gdn, w_out_b, l)
        x = _ffn(x2.reshape(B, T, D_MODEL), ffn_norm[l], w_up_b, cw, cb, w_down_b, l)
    return x
```

```python
import functools

import numpy as np
import jax
import jax.numpy as jnp
from jax import lax
from jax.experimental import pallas as pl
from jax.experimental.pallas import tpu as pltpu

D_MODEL = 1024
NSA_HEADS = 8
NSA_KV_GROUPS = 2
NSA_HPG = NSA_HEADS // NSA_KV_GROUPS
NSA_HEAD_DIM = 64
N_BRANCH = 3
CMP_BLOCK = 32
CMP_STRIDE = 16
CMP_HIDDEN = 256
SLC_BLOCK = 64
SLC_TOP_N = 16
WINDOW = 512
ROPE_THETA = 500000.0
ROPE_DIM = NSA_HEAD_DIM // 4
GDN_HEADS = 4
GDN_HEAD_DIM = 128
GDN_CONV = 4
GDN_CHUNK = 64
NSA_WIDTH = NSA_HEADS * NSA_HEAD_DIM
GDN_WIDTH = GDN_HEADS * GDN_HEAD_DIM
NSA_KV_WIDTH = N_BRANCH * 2 * NSA_KV_GROUPS * NSA_HEAD_DIM
NSA_GATE_WIDTH = NSA_HEADS * N_BRANCH
D_FF = 2816
FFN_CONV = 3
NORM_EPS = 1e-6

LANES = 128
SUBLANES = 8
SMALL_WIDTH = LANES
GDN_A_COL = NSA_GATE_WIDTH
GDN_B_COL = NSA_GATE_WIDTH + GDN_HEADS
VMEM_LIMIT = 56 * 1024 * 1024
NEG = -1e30
LOG2E = 1.4426950408889634

F32 = jnp.float32
BF16 = jnp.bfloat16


def _cparams(*sem):
    return pltpu.CompilerParams(dimension_semantics=sem, vmem_limit_bytes=VMEM_LIMIT)


def _resident(shape):
    nd = len(shape)
    return pl.BlockSpec(shape, lambda *_: (0,) * nd, pipeline_mode=pl.Buffered(1))


def _silu(x):
    return x * (1.0 / (1.0 + jnp.exp(-x)))


def _sigmoid(x):
    return 1.0 / (1.0 + jnp.exp(-x))


def _rope_kernel(pos_ref, inv_ref, c_ref, s1_ref, s2_ref):
    ang = pos_ref[...].astype(F32) * inv_ref[...]
    lane = lax.broadcasted_iota(jnp.int32, ang.shape, 1) % NSA_HEAD_DIM
    half = ROPE_DIM // 2
    cos = jnp.cos(ang)
    sin = jnp.sin(ang)
    c_ref[...] = jnp.where(lane < ROPE_DIM, cos, 1.0)
    s1_ref[...] = jnp.where((lane >= half) & (lane < ROPE_DIM), sin, 0.0)
    s2_ref[...] = jnp.where(lane < half, -sin, 0.0)


def _rope_tables(positions):
    m = positions.size
    tm = min(m, 1024)
    inv = ROPE_THETA ** (-jnp.arange(0, ROPE_DIM, 2, dtype=F32) / ROPE_DIM)
    lane = np.arange(LANES) % NSA_HEAD_DIM
    inv_row = jnp.where(lane < ROPE_DIM, inv[lane % (ROPE_DIM // 2)], 0.0).reshape(1, LANES)
    tab = jax.ShapeDtypeStruct((m, LANES), F32)
    return pl.pallas_call(
        _rope_kernel,
        grid=(m // tm,),
        in_specs=[pl.BlockSpec((tm, 1), lambda i: (i, 0)), _resident((1, LANES))],
        out_specs=[pl.BlockSpec((tm, LANES), lambda i: (i, 0))] * 3,
        out_shape=[tab] * 3,
        compiler_params=_cparams("parallel"),
        name="rope_tables",
    )(positions.reshape(m, 1), inv_row)


IN_SEG = (NSA_WIDTH, NSA_KV_WIDTH, 3 * GDN_WIDTH, GDN_WIDTH, SMALL_WIDTH)


def _in_proj_kernel(x_ref, g_ref, w_ref, q_ref, kv_ref, gq_ref, z_ref, sm_ref):
    x = x_ref[...]
    h = x * lax.rsqrt(jnp.mean(x * x, axis=-1, keepdims=True) + NORM_EPS) * g_ref[...]
    h = h.astype(BF16)
    off = 0
    for ref, width in zip((q_ref, kv_ref, gq_ref, z_ref, sm_ref), IN_SEG):
        ref[...] = jnp.dot(h, w_ref[0, :, off:off + width], preferred_element_type=F32)
        off += width


def _regroup_w_in(w_in):
    o_q = 0
    o_kv = o_q + NSA_WIDTH
    o_g = o_kv + NSA_KV_WIDTH
    o_gq = o_g + NSA_GATE_WIDTH
    o_a = o_gq + 3 * GDN_WIDTH
    o_b = o_a + GDN_HEADS
    o_z = o_b + GDN_HEADS
    pad = SMALL_WIDTH - NSA_GATE_WIDTH - 2 * GDN_HEADS
    parts = [w_in[..., o_q:o_g], w_in[..., o_gq:o_a], w_in[..., o_z:o_z + GDN_WIDTH],
             w_in[..., o_g:o_gq], w_in[..., o_a:o_z],
             jnp.zeros(w_in.shape[:-1] + (pad,), w_in.dtype)]
    return jnp.concatenate(parts, axis=-1).astype(BF16)


def _in_proj(x2, gain, w_all, layer, tm=512):
    m = x2.shape[0]
    tm = min(tm, m)
    npad = sum(IN_SEG)
    outs = [jax.ShapeDtypeStruct((m, w), F32) for w in IN_SEG]
    return pl.pallas_call(
        _in_proj_kernel,
        grid=(m // tm,),
        in_specs=[pl.BlockSpec((tm, D_MODEL), lambda i: (i, 0)),
                  _resident((1, D_MODEL)),
                  pl.BlockSpec((1, D_MODEL, npad), lambda i: (layer, 0, 0), pipeline_mode=pl.Buffered(1))],
        out_specs=[pl.BlockSpec((tm, w), lambda i: (i, 0)) for w in IN_SEG],
        out_shape=outs,
        compiler_params=_cparams("parallel"),
        name="in_proj",
    )(x2, gain.reshape(1, D_MODEL), w_all)


def _head_rms(x, gain_row):
    lane = lax.broadcasted_iota(jnp.int32, x.shape, 1)
    x2 = x * x
    left = jnp.sum(jnp.where(lane < NSA_HEAD_DIM, x2, 0.0), axis=-1, keepdims=True)
    right = jnp.sum(jnp.where(lane >= NSA_HEAD_DIM, x2, 0.0), axis=-1, keepdims=True)
    ms = jnp.where(lane < NSA_HEAD_DIM, left, right) * (1.0 / NSA_HEAD_DIM)
    return x * lax.rsqrt(ms + NORM_EPS) * gain_row


def _rope(x, c, s1, s2):
    half = ROPE_DIM // 2
    return x * c + pltpu.roll(x, half, 1) * s1 + pltpu.roll(x, LANES - half, 1) * s2


V_ROWS = NSA_HEAD_DIM + 16


def _values_with_ones(v):
    vt = v.T
    tm = v.shape[0]
    extra = jnp.where(lax.broadcasted_iota(jnp.int32, (V_ROWS - NSA_HEAD_DIM, tm), 0) == 0, 1.0, 0.0)
    parts = []
    for g in range(NSA_KV_GROUPS):
        parts += [vt[g * NSA_HEAD_DIM:(g + 1) * NSA_HEAD_DIM], extra]
    return jnp.concatenate(parts, axis=0).astype(BF16)


def _nsa_prep_kernel(q_ref, kv_ref, c_ref, s1_ref, s2_ref, qg_ref, kg_ref,
                     qt_ref, kc_ref, ks_ref, vst_ref, kw_ref, vwt_ref):
    c, s1, s2 = c_ref[...], s1_ref[...], s2_ref[...]
    qg = qg_ref[...]
    tm = c.shape[0]
    for j in range(NSA_WIDTH // LANES):
        x = q_ref[0, :, j * LANES:(j + 1) * LANES]
        y = _rope(_head_rms(x, qg), c, s1, s2) * (NSA_HEAD_DIM ** -0.5) * LOG2E
        qt_ref[0, j * LANES:(j + 1) * LANES, :] = y.T.astype(BF16)
    kv = lambda i: kv_ref[0, :, i * LANES:(i + 1) * LANES]
    kc_ref[0] = _rope(kv(0), c, s1, s2)
    ks = _rope(_head_rms(kv(2), kg_ref[1:2, :]), c, s1, s2)
    lane = lax.broadcasted_iota(jnp.int32, (tm, LANES), 1)
    blk = (pl.program_id(1) * tm + lax.broadcasted_iota(jnp.int32, (tm, LANES), 0)) // SLC_BLOCK
    onehot = jnp.where(lane - NSA_HEAD_DIM == blk, 1.0, 0.0)
    ks_ref[0, :, :LANES] = jnp.where(lane < NSA_HEAD_DIM, ks, onehot).astype(BF16)
    ks_ref[0, :, LANES:] = jnp.where(lane < NSA_HEAD_DIM, pltpu.roll(ks, NSA_HEAD_DIM, 1), onehot).astype(BF16)
    vst_ref[0] = _values_with_ones(kv(3))
    kw_ref[0] = _rope(_head_rms(kv(4), kg_ref[2:3, :]), c, s1, s2).astype(BF16)
    vwt_ref[0] = _values_with_ones(kv(5))


def _nsa_prep(q_raw, kv_raw, tabs, q_gain, k_gain, B, T, tm=512):
    tm = min(tm, T)
    nt = T // tm
    c, s1, s2 = tabs
    qg = jnp.tile(q_gain.reshape(1, NSA_HEAD_DIM), (1, 2))
    kg = jnp.tile(k_gain, (1, 2))
    kg = jnp.concatenate([kg, jnp.zeros((5, LANES), F32)], axis=0)
    tok = lambda w: pl.BlockSpec((1, tm, w), lambda b, i: (b, i, 0))
    tab = pl.BlockSpec((tm, LANES), lambda b, i: (b * nt + i, 0))
    tr = pl.BlockSpec((1, NSA_KV_GROUPS * V_ROWS, tm), lambda b, i: (b, 0, i))
    return pl.pallas_call(
        _nsa_prep_kernel,
        grid=(B, nt),
        in_specs=[tok(NSA_WIDTH), tok(NSA_KV_WIDTH), tab, tab, tab,
                  _resident((1, LANES)), _resident((8, LANES))],
        out_specs=[pl.BlockSpec((1, NSA_WIDTH, tm), lambda b, i: (b, 0, i)),
                   tok(LANES), tok(2 * LANES), tr, tok(LANES), tr],
        out_shape=[jax.ShapeDtypeStruct((B, NSA_WIDTH, T), BF16),
                   jax.ShapeDtypeStruct((B, T, LANES), F32),
                   jax.ShapeDtypeStruct((B, T, 2 * LANES), BF16),
                   jax.ShapeDtypeStruct((B, NSA_KV_GROUPS * V_ROWS, T), BF16),
                   jax.ShapeDtypeStruct((B, T, LANES), BF16),
                   jax.ShapeDtypeStruct((B, NSA_KV_GROUPS * V_ROWS, T), BF16)],
        compiler_params=_cparams("parallel", "parallel"),
        name="nsa_prep",
    )(q_raw.reshape(B, T, NSA_WIDTH), kv_raw.reshape(B, T, NSA_KV_WIDTH), c, s1, s2, qg, kg)


CMP_ROW = CMP_STRIDE * LANES
CMP_HID2 = NSA_KV_GROUPS * CMP_HIDDEN


def _compress_kernel(r_ref, pe_ref, w1_ref, w2_ref, kg_ref, t_ref, n_ref):
    is_k = pl.program_id(0) == 0
    r = r_ref[0, 0]
    ha = jnp.dot((r + pe_ref[0, 0:1, :]).astype(BF16), w1_ref[0, 0], preferred_element_type=F32)
    hb = jnp.dot((r + pe_ref[0, 1:2, :]).astype(BF16), w1_ref[0, 1], preferred_element_type=F32)
    n16 = r.shape[0]
    hid = ha + pltpu.roll(hb, n16 - 1, 0)
    y = jnp.dot(_silu(hid).astype(BF16), w2_ref[0], preferred_element_type=F32)
    y = jnp.where(is_k, _head_rms(y, kg_ref[...]), y)
    t_ref[0] = y.T.astype(BF16)
    n_ref[0] = y.astype(BF16)


def _compress_weights(cmp_pe, cmp_w1, cmp_w2):
    L = cmp_w1.shape[0]
    G, DH, H = NSA_KV_GROUPS, NSA_HEAD_DIM, CMP_HIDDEN
    w1 = cmp_w1.reshape(L, 2, 2, CMP_STRIDE, DH, H)
    eye = jnp.eye(G, dtype=cmp_w1.dtype)
    w1g = jnp.einsum('lkspdh,gf->lkspgdfh', w1, eye)
    w1g = w1g.reshape(L, 2, 2, CMP_ROW, G * H).astype(BF16)
    w2g = jnp.einsum('lkhd,gf->lkghfd', cmp_w2, eye).reshape(L, 2, G * H, G * DH).astype(BF16)
    pe = cmp_pe.reshape(L, 2, 2, CMP_STRIDE, 1, DH)
    pe = jnp.broadcast_to(pe, (L, 2, 2, CMP_STRIDE, G, DH)).reshape(L, 2, 2, CMP_ROW)
    return pe, w1g, w2g


def _compress(kc_in, kv_raw3, pe, w1g, w2g, k_gain0, layer, B, T):
    n16 = T // CMP_STRIDE
    rk = kc_in.reshape(B, n16, CMP_ROW)
    rv = kv_raw3[:, :, LANES:2 * LANES].reshape(B, n16, CMP_ROW)
    r = jnp.stack([rk, rv], axis=0)
    kg = jnp.tile(k_gain0.reshape(1, NSA_HEAD_DIM), (1, 2))
    return pl.pallas_call(
        _compress_kernel,
        grid=(2, B),
        in_specs=[pl.BlockSpec((1, 1, n16, CMP_ROW), lambda s, b: (s, b, 0, 0)),
                  pl.BlockSpec((1, 2, CMP_ROW), lambda s, b: (layer * 2 + s, 0, 0)),
                  pl.BlockSpec((1, 2, CMP_ROW, CMP_HID2), lambda s, b: (layer * 2 + s, 0, 0, 0)),
                  pl.BlockSpec((1, CMP_HID2, LANES), lambda s, b: (layer * 2 + s, 0, 0)),
                  _resident((1, LANES))],
        out_specs=[pl.BlockSpec((1, LANES, n16), lambda s, b: (s * B + b, 0, 0)),
                   pl.BlockSpec((1, n16, LANES), lambda s, b: (s * B + b, 0, 0))],
        out_shape=[jax.ShapeDtypeStruct((2 * B, LANES, n16), BF16),
                   jax.ShapeDtypeStruct((2 * B, n16, LANES), BF16)],
        compiler_params=_cparams("arbitrary", "parallel"),
        name="compress",
    )(r, pe.reshape(-1, 2, CMP_ROW), w1g.reshape(-1, 2, CMP_ROW, CMP_HID2),
      w2g.reshape(-1, CMP_HID2, LANES), kg)


ATT_TQ = 128
ATT_TK = 512
WIN_KEYS = WINDOW + ATT_TQ
SLC_MASK_DIM = 64


def _softmax_cols(s):
    m = jnp.max(s, axis=0, keepdims=True)
    p = jnp.where(m > 0.5 * NEG, jnp.exp2(s - m), 0.0)
    return p, 1.0 / jnp.maximum(jnp.sum(p, axis=0, keepdims=True), 1e-30)


def _flash_update(s, m, acc, v_ones, fixed_max):
    if fixed_max:
        p = jnp.exp2(s - m).astype(BF16)
        return m, acc + jnp.dot(v_ones, p, preferred_element_type=F32)
    m_new = jnp.maximum(m, jnp.max(s, axis=0, keepdims=True))
    p = jnp.exp2(s - m_new).astype(BF16)
    acc = acc * jnp.exp2(m - m_new) + jnp.dot(v_ones, p, preferred_element_type=F32)
    return m_new, acc


def _flash_finish(acc):
    dh = NSA_HEAD_DIM
    return acc[:dh] * (1.0 / jnp.maximum(acc[dh:dh + 1], 1e-30))


def _nsa_attn_kernel(qt_ref, gt_ref, kc_ref, vct_ref, ks_ref, vst_ref, kw_ref, vwt_ref,
                     ovt_ref, ogt_ref, mb_ref, o_ref, *, n_cmp_pad, fixed_max):
    TQ, TK, H, DH, NB = ATT_TQ, ATT_TK, NSA_HPG, NSA_HEAD_DIM, SLC_MASK_DIM
    t0 = pl.program_id(1) * TQ
    heads = lambda a: jnp.concatenate([a] * H, axis=1)
    t_row = t0 + lax.broadcasted_iota(jnp.int32, (1, TQ), 1)

    ci = lax.broadcasted_iota(jnp.int32, (n_cmp_pad, TQ), 0)
    bias_c = heads(jnp.where(ci * CMP_STRIDE + (CMP_BLOCK - 1) <= t_row, 0.0, NEG))
    w0 = pl.multiple_of(jnp.maximum(t0 - WINDOW, 0), LANES)
    rel = t_row - (w0 + lax.broadcasted_iota(jnp.int32, (WIN_KEYS, TQ), 0))
    bias_w = heads(jnp.where((rel >= 0) & (rel < WINDOW), 0.0, NEG))
    kd = t0 // TK
    k0d = pl.multiple_of(kd * TK, TK)
    bias_d = heads(jnp.where(k0d + lax.broadcasted_iota(jnp.int32, (TK, TQ), 0) <= t_row, 0.0, NEG))

    jj = lax.broadcasted_iota(jnp.int32, (NB, TQ), 0)
    cur = jnp.broadcast_to(t_row // SLC_BLOCK, (NB, TQ))
    forced = (jj == 0) | (jj == cur) | (jj == cur - 1)
    j8 = lax.broadcasted_iota(jnp.int32, (SUBLANES, TQ), 0)

    sg_t = gt_ref[0].T
    zeros_q = jnp.zeros((DH, TQ), BF16)
    G = NSA_KV_GROUPS
    rhs_q, rhs_s, o_c = [], [], []

    for g in range(G):
        lo, hi = g * DH, (g + 1) * DH
        q_t = [qt_ref[0, (g * H + h) * DH:(g * H + h + 1) * DH, :] for h in range(H)]
        pad = (lambda x: jnp.concatenate([x, zeros_q], axis=0)) if g == 0 else \
              (lambda x: jnp.concatenate([zeros_q, x], axis=0))
        rhs_q.append(jnp.concatenate([pad(x) for x in q_t], axis=1))

        pc, inv_c = _softmax_cols(jnp.dot(kc_ref[0], rhs_q[g], preferred_element_type=F32) + bias_c)
        o_c.append(jnp.dot(vct_ref[0, lo:hi, :], pc.astype(BF16), preferred_element_type=F32) * inv_c)

        psum = pc[:, :TQ] * inv_c[:, :TQ]
        for h in range(1, H):
            psum = psum + pc[:, h * TQ:(h + 1) * TQ] * inv_c[:, h * TQ:(h + 1) * TQ]
        imp = jnp.dot(ovt_ref[...], psum, preferred_element_type=F32, precision=lax.Precision.HIGHEST)
        val = jnp.where(forced, jnp.inf, jnp.where(jj > cur, -jnp.inf, imp))
        vals = [val[SUBLANES * v:SUBLANES * (v + 1)] for v in range(NB // SUBLANES)]
        cnts = [jnp.zeros((SUBLANES, TQ), F32) for _ in vals]
        for jp in range(NB):
            row = val[jp:jp + 1, :]
            for v in range(len(vals)):
                if SUBLANES * v > jp:
                    beat = jnp.where(row >= vals[v], 1.0, 0.0)
                elif SUBLANES * v + SUBLANES - 1 < jp:
                    beat = jnp.where(row > vals[v], 1.0, 0.0)
                else:
                    beat = jnp.where(j8 > jp - SUBLANES * v, jnp.where(row >= vals[v], 1.0, 0.0),
                                     jnp.where(row > vals[v], 1.0, 0.0))
                cnts[v] = cnts[v] + beat
        cnt = jnp.concatenate(cnts, axis=0)
        mask_t = jnp.where((cnt < float(SLC_TOP_N)) & (jj <= cur), 0.0, NEG).astype(BF16)

        rhs_s.append(jnp.concatenate([jnp.concatenate([x, mask_t], axis=0) for x in q_t], axis=1))

    def slc_tile(k0, carry, bias):
        out = []
        for g in range(G):
            s = jnp.dot(ks_ref[0, pl.ds(k0, TK), g * LANES:(g + 1) * LANES], rhs_s[g], preferred_element_type=F32)
            if bias is not None:
                s = s + bias
            out.append(_flash_update(s, *carry[g], vst_ref[0, g * V_ROWS:(g + 1) * V_ROWS, pl.ds(k0, TK)],
                                     fixed_max))
        return tuple(out)

    acc0 = jnp.zeros((V_ROWS, H * TQ), F32)
    m0_s = mb_ref[0:1, :] if fixed_max else jnp.full((1, H * TQ), NEG, F32)
    m0_w = mb_ref[1:2, :] if fixed_max else jnp.full((1, H * TQ), NEG, F32)
    carry = lax.fori_loop(0, kd, lambda kt, c: slc_tile(pl.multiple_of(kt * TK, TK), c, None), ((m0_s, acc0),) * G)
    carry = slc_tile(k0d, carry, bias_d)

    out_rows = []
    for g in range(G):
        o_s = _flash_finish(carry[g][1])

        sw = jnp.dot(kw_ref[0, pl.ds(w0, WIN_KEYS), :], rhs_q[g], preferred_element_type=F32) + bias_w
        o_w = _flash_finish(_flash_update(sw, m0_w, acc0, vwt_ref[0, g * V_ROWS:(g + 1) * V_ROWS,
                                                                  pl.ds(w0, WIN_KEYS)], fixed_max)[1])

        def gate(br):
            rows = [(g * H + h) * N_BRANCH + br for h in range(H)]
            return jnp.concatenate([sg_t[r:r + 1, :] for r in rows], axis=1)

        o = gate(0) * o_c[g] + gate(1) * o_s + gate(2) * o_w
        o = o * lax.rsqrt(jnp.mean(o * o, axis=0, keepdims=True) + NORM_EPS)
        for h in range(H):
            out_rows.append(o[:, h * TQ:(h + 1) * TQ] * ogt_ref[:, g * H + h:g * H + h + 1])

    o_ref[0] = jnp.concatenate(out_rows, axis=0).T


def _nsa_consts(T):
    n_cmp = (T - CMP_BLOCK) // CMP_STRIDE + 1
    n_slc = T // SLC_BLOCK
    n_cmp_pad = T // CMP_STRIDE
    cs = np.arange(n_cmp_pad) * CMP_STRIDE
    ss = np.arange(SLC_MASK_DIM) * SLC_BLOCK
    ov = ((cs[None, :] < ss[:, None] + SLC_BLOCK) & (cs[None, :] + CMP_BLOCK > ss[:, None]))
    ov = ov & (np.arange(n_cmp_pad)[None, :] < n_cmp) & (np.arange(SLC_MASK_DIM)[:, None] < n_slc)
    return jnp.asarray(ov, F32), n_cmp_pad


FIXED_MAX_LIMIT = 40.0


def _score_bound(q_gain, k_gain):
    return 1.01 * LOG2E * (NSA_HEAD_DIM ** 0.5) * jnp.max(jnp.abs(q_gain)) * jnp.max(jnp.abs(k_gain))


def _nsa_attn(qt, gates, kc, vct, ks, vst, kw, vwt, out_gain, q_gain, k_gain, B, T):
    ovt, n_cmp_pad = _nsa_consts(T)
    assert T // SLC_BLOCK <= SLC_MASK_DIM and T % ATT_TK == 0 and T >= WIN_KEYS
    nq = T // ATT_TQ
    ogt = jnp.zeros((NSA_HEAD_DIM, LANES), F32).at[:, :NSA_HEADS].set(out_gain.T)
    bounds = jnp.stack([_score_bound(q_gain, k_gain[1]), _score_bound(q_gain, k_gain[2])])
    mb = jnp.zeros((SUBLANES, NSA_HPG * ATT_TQ), F32).at[:2].set(bounds[:, None])
    per_b = lambda s1, s2: pl.BlockSpec((1, s1, s2), lambda b, i: (b, 0, 0))

    def call(fixed_max):
        return pl.pallas_call(
            functools.partial(_nsa_attn_kernel, n_cmp_pad=n_cmp_pad, fixed_max=fixed_max),
            grid=(B, nq),
            in_specs=[pl.BlockSpec((1, NSA_WIDTH, ATT_TQ), lambda b, i: (b, 0, i)),
                      pl.BlockSpec((1, ATT_TQ, SMALL_WIDTH), lambda b, i: (b, i, 0)),
                      per_b(n_cmp_pad, LANES), per_b(LANES, n_cmp_pad),
                      per_b(T, 2 * LANES), per_b(NSA_KV_GROUPS * V_ROWS, T),
                      per_b(T, LANES), per_b(NSA_KV_GROUPS * V_ROWS, T),
                      _resident((SLC_MASK_DIM, n_cmp_pad)), _resident((NSA_HEAD_DIM, LANES)),
                      _resident((SUBLANES, NSA_HPG * ATT_TQ))],
            out_specs=pl.BlockSpec((1, ATT_TQ, NSA_WIDTH), lambda b, i: (b, i, 0)),
            out_shape=jax.ShapeDtypeStruct((B, T, NSA_WIDTH), F32),
            compiler_params=_cparams("parallel", "parallel"),
            name="nsa_attn_fixed" if fixed_max else "nsa_attn_online",
        )

    args = (qt, gates, kc, vct, ks, vst, kw, vwt, ovt, ogt, mb)
    return lax.cond(jnp.max(bounds) <= FIXED_MAX_LIMIT, lambda a: call(True)(*a), lambda a: call(False)(*a), args)


def _shift_rows(x, hist, k):
    xr = pltpu.roll(x, k, 0)
    hr = pltpu.roll(hist, k, 0)
    row = lax.broadcasted_iota(jnp.int32, hist.shape, 0)
    top = jnp.where(row < k, hr, xr[:SUBLANES])
    return jnp.concatenate([top, xr[SUBLANES:]], axis=0)


def _causal_conv(x, hist, w_ref, width):
    y = x * w_ref[width - 1:width, :]
    for k in range(1, width):
        y = y + _shift_rows(x, hist, k) * w_ref[width - 1 - k:width - k, :]
    return y


def _gdn_prep_kernel(x_ref, sm_ref, w_ref, alog_ref, dtb_ref,
                     q_ref, k_ref, v_ref, sg_ref, gl_ref, hist_ref):
    @pl.when(pl.program_id(1) == 0)
    def _():
        hist_ref[...] = jnp.zeros_like(hist_ref)

    x = x_ref[0]
    y = _silu(_causal_conv(x, hist_ref[...], w_ref, GDN_CONV))
    hist_ref[...] = x[x.shape[0] - SUBLANES:, :]
    for h in range(GDN_HEADS):
        for src, dst, scale in ((0, q_ref, GDN_HEAD_DIM ** -0.5), (1, k_ref, None)):
            c0 = src * GDN_WIDTH + h * GDN_HEAD_DIM
            t = y[:, c0:c0 + GDN_HEAD_DIM]
            t = t * lax.rsqrt(jnp.sum(t * t, axis=-1, keepdims=True) + NORM_EPS)
            if scale is not None:
                t = t * scale
            dst[0, :, h * GDN_HEAD_DIM:(h + 1) * GDN_HEAD_DIM] = t
    v_ref[0] = y[:, 2 * GDN_WIDTH:]
    sm = sm_ref[0]
    sg_ref[0] = _sigmoid(sm)
    a = sm + dtb_ref[...]
    softplus = jnp.maximum(a, 0.0) + jnp.log(1.0 + jnp.exp(-jnp.abs(a)))
    gl_ref[0] = -jnp.exp(alog_ref[...]) * softplus


def _gdn_prep(gqkv, small, conv_w, a_log, dt_bias, B, T, tm=256):
    tm = min(tm, T)
    w = jnp.concatenate([conv_w, jnp.zeros((SUBLANES - GDN_CONV, 3 * GDN_WIDTH), F32)], axis=0)
    place = lambda v: jnp.zeros((1, SMALL_WIDTH), F32).at[0, GDN_A_COL:GDN_A_COL + GDN_HEADS].set(v)
    tok = lambda c: pl.BlockSpec((1, tm, c), lambda b, i: (b, i, 0))
    return pl.pallas_call(
        _gdn_prep_kernel,
        grid=(B, T // tm),
        in_specs=[tok(3 * GDN_WIDTH), tok(SMALL_WIDTH), _resident((SUBLANES, 3 * GDN_WIDTH)),
                  _resident((1, SMALL_WIDTH)), _resident((1, SMALL_WIDTH))],
        out_specs=[tok(GDN_WIDTH), tok(GDN_WIDTH), tok(GDN_WIDTH), tok(SMALL_WIDTH), tok(SMALL_WIDTH)],
        out_shape=[jax.ShapeDtypeStruct((B, T, GDN_WIDTH), F32)] * 3
                  + [jax.ShapeDtypeStruct((B, T, SMALL_WIDTH), F32)] * 2,
        scratch_shapes=[pltpu.VMEM((SUBLANES, 3 * GDN_WIDTH), F32)],
        compiler_params=_cparams("parallel", "arbitrary"),
        name="gdn_prep",
    )(gqkv.reshape(B, T, 3 * GDN_WIDTH), small.reshape(B, T, SMALL_WIDTH), w, place(a_log), place(dt_bias))


GDN_NC = 8


def _bmm(a, b):
    return jnp.einsum('cij,cjk->cik', a.astype(BF16), b.astype(BF16), preferred_element_type=F32)


def _bmm_nt(a, b):
    return jnp.einsum('cid,cjd->cij', a.astype(BF16), b.astype(BF16), preferred_element_type=F32)


def _unit_lower_inverse(a, ii, jj):
    eye = (ii == jj).astype(F32)
    blk = lambda n: (ii // n) == (jj // n)
    ad = jnp.where(blk(16), a, 0.0)
    t = eye - ad
    p = ad
    for _ in range(3):
        p = _bmm(p, p)
        t = _bmm(t, eye + p)
    for n in (32, 64):
        off = jnp.where(blk(n) & ~blk(n // 2), a, 0.0)
        t = t - _bmm(_bmm(t, off), t)
    return t


def _gdn_scan_kernel(q_ref, k_ref, v_ref, sg_ref, gl_ref, z_ref, og_ref, o_ref,
                     s_ref, u_scr, wq_scr, akd_scr, egl_scr, o_scr):
    C, DK, NC, H = GDN_CHUNK, GDN_HEAD_DIM, GDN_NC, GDN_HEADS

    @pl.when(pl.program_id(1) == 0)
    def _():
        s_ref[...] = jnp.zeros_like(s_ref)

    ii = lax.broadcasted_iota(jnp.int32, (NC, C, C), 1)
    jj = lax.broadcasted_iota(jnp.int32, (NC, C, C), 2)
    causal = ii >= jj
    sg = sg_ref[0]
    gl = gl_ref[0]

    for h in range(H):
        sl = slice(h * DK, (h + 1) * DK)
        q3 = q_ref[0, :, sl].reshape(NC, C, DK)
        k3 = k_ref[0, :, sl].reshape(NC, C, DK)
        v3 = v_ref[0, :, sl].reshape(NC, C, DK)
        g_col = gl[:, GDN_A_COL + h:GDN_A_COL + h + 1].reshape(NC, C, 1)
        b_col = sg[:, GDN_B_COL + h:GDN_B_COL + h + 1].reshape(NC, C, 1)
        g_mat = jnp.broadcast_to(g_col, (NC, C, C))
        g_row = jnp.sum(jnp.where(ii == jj, g_mat, 0.0), axis=1, keepdims=True)
        gc_col = jnp.sum(jnp.where(causal, jnp.broadcast_to(g_row, (NC, C, C)), 0.0), axis=2, keepdims=True)
        gc_row = jnp.sum(jnp.where(ii <= jj, g_mat, 0.0), axis=1, keepdims=True)
        decay = jnp.where(causal, jnp.exp(jnp.where(causal, gc_col - gc_row, 0.0)), 0.0)
        kb = k3 * b_col
        kq = _bmm_nt(jnp.concatenate([kb, q3], axis=1), k3)
        a_kk = jnp.where(ii > jj, kq[:, :C] * decay, 0.0)
        a_qk = kq[:, C:] * decay
        t_inv = _unit_lower_inverse(a_kk, ii, jj)
        egc = jnp.exp(gc_col)
        uw = _bmm(t_inv, jnp.concatenate([v3 * b_col, kb * egc], axis=2))
        g_last = gc_col[:, C - 1:C, :]
        kd = k3 * jnp.exp(g_last - gc_col)
        u_scr[h] = uw[:, :, :DK]
        wq_scr[h] = jnp.concatenate([uw[:, :, DK:], q3 * egc], axis=1).astype(BF16)
        akd_scr[h] = jnp.concatenate([a_qk, jnp.swapaxes(kd, 1, 2)], axis=1).astype(BF16)
        egl_scr[h] = jnp.broadcast_to(jnp.exp(g_last), (NC, SUBLANES, DK))

    def step(c, carry):
        r0 = pl.multiple_of(c * C, C)
        for h in range(H):
            s = s_ref[h]
            r = jnp.dot(wq_scr[h, c], s.astype(BF16), preferred_element_type=F32)
            v_new = (u_scr[h, c] - r[:C]).astype(BF16)
            r2 = jnp.dot(akd_scr[h, c], v_new, preferred_element_type=F32)
            o_scr[pl.ds(r0, C), h * DK:(h + 1) * DK] = r[C:] + r2[:C]
            s_ref[h] = s * egl_scr[h, c][0:1, :] + r2[C:]
        return carry

    lax.fori_loop(0, NC, step, 0)

    for h in range(H):
        sl = slice(h * DK, (h + 1) * DK)
        o = o_scr[:, sl]
        o = o * lax.rsqrt(jnp.mean(o * o, axis=-1, keepdims=True) + NORM_EPS) * og_ref[...]
        o_ref[0, :, sl] = o * _silu(z_ref[0, :, sl])


def _gdn_scan(q, k, v, sg, gl, z, out_gain, B, T):
    C, NC, H, DK = GDN_CHUNK, GDN_NC, GDN_HEADS, GDN_HEAD_DIM
    ts = NC * C
    tok = lambda c: pl.BlockSpec((1, ts, c), lambda b, i: (b, i, 0))
    return pl.pallas_call(
        _gdn_scan_kernel,
        grid=(B, T // ts),
        in_specs=[tok(GDN_WIDTH), tok(GDN_WIDTH), tok(GDN_WIDTH), tok(SMALL_WIDTH), tok(SMALL_WIDTH),
                  tok(GDN_WIDTH), _resident((1, GDN_HEAD_DIM))],
        out_specs=tok(GDN_WIDTH),
        out_shape=jax.ShapeDtypeStruct((B, T, GDN_WIDTH), F32),
        scratch_shapes=[pltpu.VMEM((H, DK, DK), F32),
                        pltpu.VMEM((H, NC, C, DK), F32),
                        pltpu.VMEM((H, NC, 2 * C, DK), BF16),
                        pltpu.VMEM((H, NC, C + DK, C), BF16),
                        pltpu.VMEM((H, NC, SUBLANES, DK), F32),
                        pltpu.VMEM((ts, GDN_WIDTH), F32)],
        compiler_params=_cparams("parallel", "arbitrary"),
        name="gdn_scan",
    )(q, k, v, sg, gl, z.reshape(B, T, GDN_WIDTH), out_gain.reshape(1, GDN_HEAD_DIM))


def _out_proj_kernel(x_ref, a_ref, b_ref, w_ref, o_ref):
    acc = jnp.dot(a_ref[...].astype(BF16), w_ref[0, :NSA_WIDTH, :], preferred_element_type=F32)
    acc = acc + jnp.dot(b_ref[...].astype(BF16), w_ref[0, NSA_WIDTH:, :], preferred_element_type=F32)
    o_ref[...] = x_ref[...] + acc


def _out_proj(x2, o_nsa, o_gdn, w_all, layer, tm=512):
    m = x2.shape[0]
    tm = min(tm, m)
    row = lambda c: pl.BlockSpec((tm, c), lambda i: (i, 0))
    return pl.pallas_call(
        _out_proj_kernel,
        grid=(m // tm,),
        in_specs=[row(D_MODEL), row(NSA_WIDTH), row(GDN_WIDTH),
                  pl.BlockSpec((1, NSA_WIDTH + GDN_WIDTH, D_MODEL), lambda i: (layer, 0, 0),
                               pipeline_mode=pl.Buffered(1))],
        out_specs=row(D_MODEL),
        out_shape=jax.ShapeDtypeStruct((m, D_MODEL), F32),
        compiler_params=_cparams("parallel"),
        name="out_proj",
    )(x2, o_nsa.reshape(m, NSA_WIDTH), o_gdn.reshape(m, GDN_WIDTH), w_all)


FFN_CHUNK = 256


def _ffn_kernel(x_ref, g_ref, wu_ref, cw_ref, cb_ref, wd_ref, o_ref, hist_ref):
    @pl.when(pl.program_id(1) == 0)
    def _():
        hist_ref[...] = jnp.zeros_like(hist_ref)

    x = x_ref[0]
    tm = x.shape[0]
    h = (x * lax.rsqrt(jnp.mean(x * x, axis=-1, keepdims=True) + NORM_EPS) * g_ref[...]).astype(BF16)
    acc = x

    def conv_cols(c0):
        u = jnp.dot(h, wu_ref[0, :, c0:c0 + FFN_CHUNK], preferred_element_type=F32)
        hist = hist_ref[:, c0:c0 + FFN_CHUNK]
        y = u * cw_ref[0, FFN_CONV - 1:FFN_CONV, c0:c0 + FFN_CHUNK]
        for k in range(1, FFN_CONV):
            y = y + _shift_rows(u, hist, k) * cw_ref[0, FFN_CONV - 1 - k:FFN_CONV - k, c0:c0 + FFN_CHUNK]
        hist_ref[:, c0:c0 + FFN_CHUNK] = u[tm - SUBLANES:, :]
        return y + cb_ref[0, :, c0:c0 + FFN_CHUNK]

    for c in range(D_FF // FFN_CHUNK):
        gate = conv_cols(c * FFN_CHUNK)
        up = conv_cols(D_FF + c * FFN_CHUNK)
        act = (_silu(gate) * up).astype(BF16)
        acc = acc + jnp.dot(act, wd_ref[0, c * FFN_CHUNK:(c + 1) * FFN_CHUNK, :], preferred_element_type=F32)
    o_ref[0] = acc


def _ffn(x3, gain, wu_all, cw_all, cb_all, wd_all, layer, tm=512):
    B, T, _ = x3.shape
    tm = min(tm, T)
    lay = lambda s1, s2: pl.BlockSpec((1, s1, s2), lambda b, i: (layer, 0, 0), pipeline_mode=pl.Buffered(1))
    return pl.pallas_call(
        _ffn_kernel,
        grid=(B, T // tm),
        in_specs=[pl.BlockSpec((1, tm, D_MODEL), lambda b, i: (b, i, 0)),
                  _resident((1, D_MODEL)),
                  lay(D_MODEL, 2 * D_FF), lay(SUBLANES, 2 * D_FF), lay(1, 2 * D_FF), lay(D_FF, D_MODEL)],
        out_specs=pl.BlockSpec((1, tm, D_MODEL), lambda b, i: (b, i, 0)),
        out_shape=jax.ShapeDtypeStruct((B, T, D_MODEL), F32),
        scratch_shapes=[pltpu.VMEM((SUBLANES, 2 * D_FF), F32)],
        compiler_params=_cparams("parallel", "arbitrary"),
        name="ffn",
    )(x3, gain.reshape(1, D_MODEL), wu_all, cw_all, cb_all, wd_all)


def kernel(x, positions, attn_norm, w_in, nsa_q_norm, nsa_k_norm, cmp_pe, cmp_w1, cmp_w2, nsa_out_norm,
           gdn_conv_w, gdn_a_log, gdn_dt_bias, gdn_out_norm, w_out, ffn_norm, w_up, ffn_conv_w,
           ffn_conv_b, w_down):
    B, T, _ = x.shape
    depth = w_in.shape[0]
    M = B * T
    tabs = _rope_tables(positions)
    w_in_r = _regroup_w_in(w_in)
    pe, w1g, w2g = _compress_weights(cmp_pe, cmp_w1, cmp_w2)
    w_out_b = w_out.astype(BF16)
    w_up_b = w_up.astype(BF16)
    w_down_b = w_down.astype(BF16)
    cw = jnp.concatenate([ffn_conv_w, jnp.zeros((depth, SUBLANES - FFN_CONV, 2 * D_FF), F32)], axis=1)
    cb = ffn_conv_b.reshape(depth, 1, 2 * D_FF)

    for l in range(depth):
        x2 = x.reshape(M, D_MODEL)
        q_raw, kv_raw, gqkv, z, small = _in_proj(x2, attn_norm[l], w_in_r, l)
        qt, kc_in, ks, vst, kw, vwt = _nsa_prep(q_raw, kv_raw, tabs, nsa_q_norm[l], nsa_k_norm[l], B, T)
        cmp_t, cmp_n = _compress(kc_in, kv_raw.reshape(B, T, NSA_KV_WIDTH), pe, w1g, w2g,
                                 nsa_k_norm[l, 0], l, B, T)
        gq, gk, gv, sg, gl = _gdn_prep(gqkv, small, gdn_conv_w[l], gdn_a_log[l], gdn_dt_bias[l], B, T)
        o_nsa = _nsa_attn(qt, sg, cmp_n[:B], cmp_t[B:], ks, vst, kw, vwt, nsa_out_norm[l],
                          nsa_q_norm[l], nsa_k_norm[l], B, T)
        o_gdn = _gdn_scan(gq, gk, gv, sg, gl, z, gdn_out_norm[l], B, T)
        x2 = _out_proj(x2, o_nsa, o_gdn, w_out_b, l)
        x = _ffn(x2.reshape(B, T, D_MODEL), ffn_norm[l], w_up_b, cw, cb, w_down_b, l)
    return x
```

```python
import functools

import numpy as np
import jax
import jax.numpy as jnp
from jax import lax
from jax.experimental import pallas as pl
from jax.experimental.pallas import tpu as pltpu

D_MODEL = 1024
NSA_HEADS = 8
NSA_KV_GROUPS = 2
NSA_HPG = NSA_HEADS // NSA_KV_GROUPS
NSA_HEAD_DIM = 64
N_BRANCH = 3
CMP_BLOCK = 32
CMP_STRIDE = 16
CMP_HIDDEN = 256
SLC_BLOCK = 64
SLC_TOP_N = 16
WINDOW = 512
ROPE_THETA = 500000.0
ROPE_DIM = NSA_HEAD_DIM // 4
GDN_HEADS = 4
GDN_HEAD_DIM = 128
GDN_CONV = 4
GDN_CHUNK = 64
NSA_WIDTH = NSA_HEADS * NSA_HEAD_DIM
GDN_WIDTH = GDN_HEADS * GDN_HEAD_DIM
NSA_KV_WIDTH = N_BRANCH * 2 * NSA_KV_GROUPS * NSA_HEAD_DIM
NSA_GATE_WIDTH = NSA_HEADS * N_BRANCH
D_FF = 2816
FFN_CONV = 3
NORM_EPS = 1e-6

LANES = 128
SUBLANES = 8
SMALL_WIDTH = LANES
GDN_A_COL = NSA_GATE_WIDTH
GDN_B_COL = NSA_GATE_WIDTH + GDN_HEADS
VMEM_LIMIT = 56 * 1024 * 1024
NEG = -1e30
LOG2E = 1.4426950408889634

F32 = jnp.float32
BF16 = jnp.bfloat16


def _cparams(*sem):
    return pltpu.CompilerParams(dimension_semantics=sem, vmem_limit_bytes=VMEM_LIMIT)


def _resident(shape):
    nd = len(shape)
    return pl.BlockSpec(shape, lambda *_: (0,) * nd, pipeline_mode=pl.Buffered(1))


def _silu(x):
    return x * (1.0 / (1.0 + jnp.exp(-x)))


def _sigmoid(x):
    return 1.0 / (1.0 + jnp.exp(-x))


def _rope_kernel(pos_ref, inv_ref, c_ref, s1_ref, s2_ref):
    ang = pos_ref[...].astype(F32) * inv_ref[...]
    lane = lax.broadcasted_iota(jnp.int32, ang.shape, 1) % NSA_HEAD_DIM
    half = ROPE_DIM // 2
    cos = jnp.cos(ang)
    sin = jnp.sin(ang)
    c_ref[...] = jnp.where(lane < ROPE_DIM, cos, 1.0)
    s1_ref[...] = jnp.where((lane >= half) & (lane < ROPE_DIM), sin, 0.0)
    s2_ref[...] = jnp.where(lane < half, -sin, 0.0)


def _rope_tables(positions):
    m = positions.size
    tm = min(m, 1024)
    inv = ROPE_THETA ** (-jnp.arange(0, ROPE_DIM, 2, dtype=F32) / ROPE_DIM)
    lane = np.arange(LANES) % NSA_HEAD_DIM
    inv_row = jnp.where(lane < ROPE_DIM, inv[lane % (ROPE_DIM // 2)], 0.0).reshape(1, LANES)
    tab = jax.ShapeDtypeStruct((m, LANES), F32)
    return pl.pallas_call(
        _rope_kernel,
        grid=(m // tm,),
        in_specs=[pl.BlockSpec((tm, 1), lambda i: (i, 0)), _resident((1, LANES))],
        out_specs=[pl.BlockSpec((tm, LANES), lambda i: (i, 0))] * 3,
        out_shape=[tab] * 3,
        compiler_params=_cparams("parallel"),
        name="rope_tables",
    )(positions.reshape(m, 1), inv_row)


IN_SEG = (NSA_WIDTH, NSA_KV_WIDTH, 3 * GDN_WIDTH, GDN_WIDTH, SMALL_WIDTH)


def _in_proj_kernel(x_ref, g_ref, w_ref, q_ref, kv_ref, gq_ref, z_ref, sm_ref):
    x = x_ref[...]
    h = x * lax.rsqrt(jnp.mean(x * x, axis=-1, keepdims=True) + NORM_EPS) * g_ref[...]
    h = h.astype(BF16)
    off = 0
    for ref, width in zip((q_ref, kv_ref, gq_ref, z_ref, sm_ref), IN_SEG):
        ref[...] = jnp.dot(h, w_ref[0, :, off:off + width], preferred_element_type=F32)
        off += width


def _regroup_w_in(w_in):
    o_q = 0
    o_kv = o_q + NSA_WIDTH
    o_g = o_kv + NSA_KV_WIDTH
    o_gq = o_g + NSA_GATE_WIDTH
    o_a = o_gq + 3 * GDN_WIDTH
    o_b = o_a + GDN_HEADS
    o_z = o_b + GDN_HEADS
    pad = SMALL_WIDTH - NSA_GATE_WIDTH - 2 * GDN_HEADS
    parts = [w_in[..., o_q:o_g], w_in[..., o_gq:o_a], w_in[..., o_z:o_z + GDN_WIDTH],
             w_in[..., o_g:o_gq], w_in[..., o_a:o_z],
             jnp.zeros(w_in.shape[:-1] + (pad,), w_in.dtype)]
    return jnp.concatenate(parts, axis=-1).astype(BF16)


def _in_proj(x2, gain, w_all, layer, tm=512):
    m = x2.shape[0]
    tm = min(tm, m)
    npad = sum(IN_SEG)
    outs = [jax.ShapeDtypeStruct((m, w), F32) for w in IN_SEG]
    return pl.pallas_call(
        _in_proj_kernel,
        grid=(m // tm,),
        in_specs=[pl.BlockSpec((tm, D_MODEL), lambda i: (i, 0)),
                  _resident((1, D_MODEL)),
                  pl.BlockSpec((1, D_MODEL, npad), lambda i: (layer, 0, 0), pipeline_mode=pl.Buffered(1))],
        out_specs=[pl.BlockSpec((tm, w), lambda i: (i, 0)) for w in IN_SEG],
        out_shape=outs,
        compiler_params=_cparams("parallel"),
        name="in_proj",
    )(x2, gain.reshape(1, D_MODEL), w_all)


def _head_rms(x, gain_row):
    lane = lax.broadcasted_iota(jnp.int32, x.shape, 1)
    x2 = x * x
    left = jnp.sum(jnp.where(lane < NSA_HEAD_DIM, x2, 0.0), axis=-1, keepdims=True)
    right = jnp.sum(jnp.where(lane >= NSA_HEAD_DIM, x2, 0.0), axis=-1, keepdims=True)
    ms = jnp.where(lane < NSA_HEAD_DIM, left, right) * (1.0 / NSA_HEAD_DIM)
    return x * lax.rsqrt(ms + NORM_EPS) * gain_row


def _rope(x, c, s1, s2):
    half = ROPE_DIM // 2
    return x * c + pltpu.roll(x, half, 1) * s1 + pltpu.roll(x, LANES - half, 1) * s2


V_ROWS = NSA_HEAD_DIM + 16


def _values_with_ones(v):
    vt = v.T
    tm = v.shape[0]
    extra = jnp.where(lax.broadcasted_iota(jnp.int32, (V_ROWS - NSA_HEAD_DIM, tm), 0) == 0, 1.0, 0.0)
    parts = []
    for g in range(NSA_KV_GROUPS):
        parts += [vt[g * NSA_HEAD_DIM:(g + 1) * NSA_HEAD_DIM], extra]
    return jnp.concatenate(parts, axis=0).astype(BF16)


def _nsa_prep_kernel(q_ref, kv_ref, c_ref, s1_ref, s2_ref, qg_ref, kg_ref,
                     qt_ref, kc_ref, ks_ref, vst_ref, kw_ref, vwt_ref):
    c, s1, s2 = c_ref[...], s1_ref[...], s2_ref[...]
    qg = qg_ref[...]
    tm = c.shape[0]
    for j in range(NSA_WIDTH // LANES):
        x = q_ref[0, :, j * LANES:(j + 1) * LANES]
        y = _rope(_head_rms(x, qg), c, s1, s2) * (NSA_HEAD_DIM ** -0.5) * LOG2E
        qt_ref[0, j * LANES:(j + 1) * LANES, :] = y.T.astype(BF16)
    kv = lambda i: kv_ref[0, :, i * LANES:(i + 1) * LANES]
    kc_ref[0] = _rope(kv(0), c, s1, s2)
    ks = _rope(_head_rms(kv(2), kg_ref[1:2, :]), c, s1, s2)
    lane = lax.broadcasted_iota(jnp.int32, (tm, LANES), 1)
    blk = (pl.program_id(1) * tm + lax.broadcasted_iota(jnp.int32, (tm, LANES), 0)) // SLC_BLOCK
    onehot = jnp.where(lane - NSA_HEAD_DIM == blk, 1.0, 0.0)
    ks_ref[0, :, :LANES] = jnp.where(lane < NSA_HEAD_DIM, ks, onehot).astype(BF16)
    ks_ref[0, :, LANES:] = jnp.where(lane < NSA_HEAD_DIM, pltpu.roll(ks, NSA_HEAD_DIM, 1), onehot).astype(BF16)
    vst_ref[0] = _values_with_ones(kv(3))
    kw_ref[0] = _rope(_head_rms(kv(4), kg_ref[2:3, :]), c, s1, s2).astype(BF16)
    vwt_ref[0] = _values_with_ones(kv(5))


def _nsa_prep(q_raw, kv_raw, tabs, q_gain, k_gain, B, T, tm=512):
    tm = min(tm, T)
    nt = T // tm
    c, s1, s2 = tabs
    qg = jnp.tile(q_gain.reshape(1, NSA_HEAD_DIM), (1, 2))
    kg = jnp.tile(k_gain, (1, 2))
    kg = jnp.concatenate([kg, jnp.zeros((5, LANES), F32)], axis=0)
    tok = lambda w: pl.BlockSpec((1, tm, w), lambda b, i: (b, i, 0))
    tab = pl.BlockSpec((tm, LANES), lambda b, i: (b * nt + i, 0))
    tr = pl.BlockSpec((1, NSA_KV_GROUPS * V_ROWS, tm), lambda b, i: (b, 0, i))
    return pl.pallas_call(
        _nsa_prep_kernel,
        grid=(B, nt),
        in_specs=[tok(NSA_WIDTH), tok(NSA_KV_WIDTH), tab, tab, tab,
                  _resident((1, LANES)), _resident((8, LANES))],
        out_specs=[pl.BlockSpec((1, NSA_WIDTH, tm), lambda b, i: (b, 0, i)),
                   tok(LANES), tok(2 * LANES), tr, tok(LANES), tr],
        out_shape=[jax.ShapeDtypeStruct((B, NSA_WIDTH, T), BF16),
                   jax.ShapeDtypeStruct((B, T, LANES), F32),
                   jax.ShapeDtypeStruct((B, T, 2 * LANES), BF16),
                   jax.ShapeDtypeStruct((B, NSA_KV_GROUPS * V_ROWS, T), BF16),
                   jax.ShapeDtypeStruct((B, T, LANES), BF16),
                   jax.ShapeDtypeStruct((B, NSA_KV_GROUPS * V_ROWS, T), BF16)],
        compiler_params=_cparams("parallel", "parallel"),
        name="nsa_prep",
    )(q_raw.reshape(B, T, NSA_WIDTH), kv_raw.reshape(B, T, NSA_KV_WIDTH), c, s1, s2, qg, kg)


CMP_ROW = CMP_STRIDE * LANES
CMP_HID2 = NSA_KV_GROUPS * CMP_HIDDEN


def _compress_kernel(r_ref, pe_ref, w1_ref, w2_ref, kg_ref, t_ref, n_ref):
    is_k = pl.program_id(0) == 0
    r = r_ref[0, 0]
    ha = jnp.dot((r + pe_ref[0, 0:1, :]).astype(BF16), w1_ref[0, 0], preferred_element_type=F32)
    hb = jnp.dot((r + pe_ref[0, 1:2, :]).astype(BF16), w1_ref[0, 1], preferred_element_type=F32)
    n16 = r.shape[0]
    hid = ha + pltpu.roll(hb, n16 - 1, 0)
    y = jnp.dot(_silu(hid).astype(BF16), w2_ref[0], preferred_element_type=F32)
    y = jnp.where(is_k, _head_rms(y, kg_ref[...]), y)
    t_ref[0] = y.T.astype(BF16)
    n_ref[0] = y.astype(BF16)


def _compress_weights(cmp_pe, cmp_w1, cmp_w2):
    L = cmp_w1.shape[0]
    G, DH, H = NSA_KV_GROUPS, NSA_HEAD_DIM, CMP_HIDDEN
    w1 = cmp_w1.reshape(L, 2, 2, CMP_STRIDE, DH, H)
    eye = jnp.eye(G, dtype=cmp_w1.dtype)
    w1g = jnp.einsum('lkspdh,gf->lkspgdfh', w1, eye)
    w1g = w1g.reshape(L, 2, 2, CMP_ROW, G * H).astype(BF16)
    w2g = jnp.einsum('lkhd,gf->lkghfd', cmp_w2, eye).reshape(L, 2, G * H, G * DH).astype(BF16)
    pe = cmp_pe.reshape(L, 2, 2, CMP_STRIDE, 1, DH)
    pe = jnp.broadcast_to(pe, (L, 2, 2, CMP_STRIDE, G, DH)).reshape(L, 2, 2, CMP_ROW)
    return pe, w1g, w2g


def _compress(kc_in, kv_raw3, pe, w1g, w2g, k_gain0, layer, B, T):
    n16 = T // CMP_STRIDE
    rk = kc_in.reshape(B, n16, CMP_ROW)
    rv = kv_raw3[:, :, LANES:2 * LANES].reshape(B, n16, CMP_ROW)
    r = jnp.stack([rk, rv], axis=0)
    kg = jnp.tile(k_gain0.reshape(1, NSA_HEAD_DIM), (1, 2))
    return pl.pallas_call(
        _compress_kernel,
        grid=(2, B),
        in_specs=[pl.BlockSpec((1, 1, n16, CMP_ROW), lambda s, b: (s, b, 0, 0)),
                  pl.BlockSpec((1, 2, CMP_ROW), lambda s, b: (layer * 2 + s, 0, 0)),
                  pl.BlockSpec((1, 2, CMP_ROW, CMP_HID2), lambda s, b: (layer * 2 + s, 0, 0, 0)),
                  pl.BlockSpec((1, CMP_HID2, LANES), lambda s, b: (layer * 2 + s, 0, 0)),
                  _resident((1, LANES))],
        out_specs=[pl.BlockSpec((1, LANES, n16), lambda s, b: (s * B + b, 0, 0)),
                   pl.BlockSpec((1, n16, LANES), lambda s, b: (s * B + b, 0, 0))],
        out_shape=[jax.ShapeDtypeStruct((2 * B, LANES, n16), BF16),
                   jax.ShapeDtypeStruct((2 * B, n16, LANES), BF16)],
        compiler_params=_cparams("arbitrary", "parallel"),
        name="compress",
    )(r, pe.reshape(-1, 2, CMP_ROW), w1g.reshape(-1, 2, CMP_ROW, CMP_HID2),
      w2g.reshape(-1, CMP_HID2, LANES), kg)


ATT_TQ = 128
ATT_TK = 512
WIN_KEYS = WINDOW + ATT_TQ
SLC_MASK_DIM = 64


def _softmax_cols(s):
    m = jnp.max(s, axis=0, keepdims=True)
    p = jnp.where(m > 0.5 * NEG, jnp.exp2(s - m), 0.0)
    return p, 1.0 / jnp.maximum(jnp.sum(p, axis=0, keepdims=True), 1e-30)


def _flash_update(s, m, acc, v_ones, fixed_max):
    if fixed_max:
        p = jnp.exp2(s - m).astype(BF16)
        return m, acc + jnp.dot(v_ones, p, preferred_element_type=F32)
    m_new = jnp.maximum(m, jnp.max(s, axis=0, keepdims=True))
    p = jnp.exp2(s - m_new).astype(BF16)
    acc = acc * jnp.exp2(m - m_new) + jnp.dot(v_ones, p, preferred_element_type=F32)
    return m_new, acc


def _flash_finish(acc):
    dh = NSA_HEAD_DIM
    return acc[:dh] * (1.0 / jnp.maximum(acc[dh:dh + 1], 1e-30))


def _nsa_attn_kernel(qt_ref, gt_ref, kc_ref, vct_ref, ks_ref, vst_ref, kw_ref, vwt_ref,
                     ovt_ref, ogt_ref, mb_ref, o_ref, *, n_cmp_pad, fixed_max):
    TQ, TK, H, DH, NB = ATT_TQ, ATT_TK, NSA_HPG, NSA_HEAD_DIM, SLC_MASK_DIM
    t0 = pl.program_id(1) * TQ
    heads = lambda a: jnp.concatenate([a] * H, axis=1)
    t_row = t0 + lax.broadcasted_iota(jnp.int32, (1, TQ), 1)

    ci = lax.broadcasted_iota(jnp.int32, (n_cmp_pad, TQ), 0)
    bias_c = heads(jnp.where(ci * CMP_STRIDE + (CMP_BLOCK - 1) <= t_row, 0.0, NEG))
    w0 = pl.multiple_of(jnp.maximum(t0 - WINDOW, 0), LANES)
    rel = t_row - (w0 + lax.broadcasted_iota(jnp.int32, (WIN_KEYS, TQ), 0))
    bias_w = heads(jnp.where((rel >= 0) & (rel < WINDOW), 0.0, NEG))
    kd = t0 // TK
    k0d = pl.multiple_of(kd * TK, TK)
    bias_d = heads(jnp.where(k0d + lax.broadcasted_iota(jnp.int32, (TK, TQ), 0) <= t_row, 0.0, NEG))

    jj = lax.broadcasted_iota(jnp.int32, (NB, TQ), 0)
    cur = jnp.broadcast_to(t_row // SLC_BLOCK, (NB, TQ))
    forced = (jj == 0) | (jj == cur) | (jj == cur - 1)
    j8 = lax.broadcasted_iota(jnp.int32, (SUBLANES, TQ), 0)

    sg_t = gt_ref[0].T
    zeros_q = jnp.zeros((DH, TQ), BF16)
    G = NSA_KV_GROUPS
    rhs_q, rhs_s, o_c = [], [], []

    for g in range(G):
        lo, hi = g * DH, (g + 1) * DH
        q_t = [qt_ref[0, (g * H + h) * DH:(g * H + h + 1) * DH, :] for h in range(H)]
        pad = (lambda x: jnp.concatenate([x, zeros_q], axis=0)) if g == 0 else \
              (lambda x: jnp.concatenate([zeros_q, x], axis=0))
        rhs_q.append(jnp.concatenate([pad(x) for x in q_t], axis=1))

        pc, inv_c = _softmax_cols(jnp.dot(kc_ref[0], rhs_q[g], preferred_element_type=F32) + bias_c)
        o_c.append(jnp.dot(vct_ref[0, lo:hi, :], pc.astype(BF16), preferred_element_type=F32) * inv_c)

        def ranked_mask(pc=pc, inv_c=inv_c):
            psum = pc[:, :TQ] * inv_c[:, :TQ]
            for h in range(1, H):
                psum = psum + pc[:, h * TQ:(h + 1) * TQ] * inv_c[:, h * TQ:(h + 1) * TQ]
            imp = jnp.dot(ovt_ref[...], psum, preferred_element_type=F32, precision=lax.Precision.HIGHEST)
            val = jnp.where(forced, jnp.inf, jnp.where(jj > cur, -jnp.inf, imp))
            vals = [val[SUBLANES * v:SUBLANES * (v + 1)] for v in range(NB // SUBLANES)]
            cnts = [jnp.zeros((SUBLANES, TQ), F32) for _ in vals]
            for jp in range(NB):
                row = val[jp:jp + 1, :]
                for v in range(len(vals)):
                    if SUBLANES * v > jp:
                        beat = jnp.where(row >= vals[v], 1.0, 0.0)
                    elif SUBLANES * v + SUBLANES - 1 < jp:
                        beat = jnp.where(row > vals[v], 1.0, 0.0)
                    else:
                        beat = jnp.where(j8 > jp - SUBLANES * v, jnp.where(row >= vals[v], 1.0, 0.0),
                                         jnp.where(row > vals[v], 1.0, 0.0))
                    cnts[v] = cnts[v] + beat
            cnt = jnp.concatenate(cnts, axis=0)
            return jnp.where((cnt < float(SLC_TOP_N)) & (jj <= cur), 0.0, NEG).astype(BF16)

        def causal_mask():
            return jnp.where(jj <= cur, 0.0, NEG).astype(BF16)

        mask_t = lax.cond(t0 + TQ > SLC_TOP_N * SLC_BLOCK, ranked_mask, causal_mask)

        rhs_s.append(jnp.concatenate([jnp.concatenate([x, mask_t], axis=0) for x in q_t], axis=1))

    def slc_tile(k0, carry, bias):
        out = []
        for g in range(G):
            s = jnp.dot(ks_ref[0, pl.ds(k0, TK), g * LANES:(g + 1) * LANES], rhs_s[g], preferred_element_type=F32)
            if bias is not None:
                s = s + bias
            out.append(_flash_update(s, *carry[g], vst_ref[0, g * V_ROWS:(g + 1) * V_ROWS, pl.ds(k0, TK)],
                                     fixed_max))
        return tuple(out)

    acc0 = jnp.zeros((V_ROWS, H * TQ), F32)
    m0_s = mb_ref[0:1, :] if fixed_max else jnp.full((1, H * TQ), NEG, F32)
    m0_w = mb_ref[1:2, :] if fixed_max else jnp.full((1, H * TQ), NEG, F32)
    carry = lax.fori_loop(0, kd, lambda kt, c: slc_tile(pl.multiple_of(kt * TK, TK), c, None), ((m0_s, acc0),) * G)
    carry = slc_tile(k0d, carry, bias_d)

    out_rows = []
    for g in range(G):
        o_s = _flash_finish(carry[g][1])

        sw = jnp.dot(kw_ref[0, pl.ds(w0, WIN_KEYS), :], rhs_q[g], preferred_element_type=F32) + bias_w
        o_w = _flash_finish(_flash_update(sw, m0_w, acc0, vwt_ref[0, g * V_ROWS:(g + 1) * V_ROWS,
                                                                  pl.ds(w0, WIN_KEYS)], fixed_max)[1])

        def gate(br):
            rows = [(g * H + h) * N_BRANCH + br for h in range(H)]
            return jnp.concatenate([sg_t[r:r + 1, :] for r in rows], axis=1)

        o = gate(0) * o_c[g] + gate(1) * o_s + gate(2) * o_w
        o = o * lax.rsqrt(jnp.mean(o * o, axis=0, keepdims=True) + NORM_EPS)
        for h in range(H):
            out_rows.append(o[:, h * TQ:(h + 1) * TQ] * ogt_ref[:, g * H + h:g * H + h + 1])

    o_ref[0] = jnp.concatenate(out_rows, axis=0).T


def _nsa_consts(T):
    n_cmp = (T - CMP_BLOCK) // CMP_STRIDE + 1
    n_slc = T // SLC_BLOCK
    n_cmp_pad = T // CMP_STRIDE
    cs = np.arange(n_cmp_pad) * CMP_STRIDE
    ss = np.arange(SLC_MASK_DIM) * SLC_BLOCK
    ov = ((cs[None, :] < ss[:, None] + SLC_BLOCK) & (cs[None, :] + CMP_BLOCK > ss[:, None]))
    ov = ov & (np.arange(n_cmp_pad)[None, :] < n_cmp) & (np.arange(SLC_MASK_DIM)[:, None] < n_slc)
    return jnp.asarray(ov, F32), n_cmp_pad


FIXED_MAX_LIMIT = 40.0


def _score_bound(q_gain, k_gain):
    return 1.01 * LOG2E * (NSA_HEAD_DIM ** 0.5) * jnp.max(jnp.abs(q_gain)) * jnp.max(jnp.abs(k_gain))


def _nsa_attn(qt, gates, kc, vct, ks, vst, kw, vwt, out_gain, q_gain, k_gain, B, T):
    ovt, n_cmp_pad = _nsa_consts(T)
    assert T // SLC_BLOCK <= SLC_MASK_DIM and T % ATT_TK == 0 and T >= WIN_KEYS
    nq = T // ATT_TQ
    ogt = jnp.zeros((NSA_HEAD_DIM, LANES), F32).at[:, :NSA_HEADS].set(out_gain.T)
    bounds = jnp.stack([_score_bound(q_gain, k_gain[1]), _score_bound(q_gain, k_gain[2])])
    mb = jnp.zeros((SUBLANES, NSA_HPG * ATT_TQ), F32).at[:2].set(bounds[:, None])
    per_b = lambda s1, s2: pl.BlockSpec((1, s1, s2), lambda b, i: (b, 0, 0))

    def call(fixed_max):
        return pl.pallas_call(
            functools.partial(_nsa_attn_kernel, n_cmp_pad=n_cmp_pad, fixed_max=fixed_max),
            grid=(B, nq),
            in_specs=[pl.BlockSpec((1, NSA_WIDTH, ATT_TQ), lambda b, i: (b, 0, i)),
                      pl.BlockSpec((1, ATT_TQ, SMALL_WIDTH), lambda b, i: (b, i, 0)),
                      per_b(n_cmp_pad, LANES), per_b(LANES, n_cmp_pad),
                      per_b(T, 2 * LANES), per_b(NSA_KV_GROUPS * V_ROWS, T),
                      per_b(T, LANES), per_b(NSA_KV_GROUPS * V_ROWS, T),
                      _resident((SLC_MASK_DIM, n_cmp_pad)), _resident((NSA_HEAD_DIM, LANES)),
                      _resident((SUBLANES, NSA_HPG * ATT_TQ))],
            out_specs=pl.BlockSpec((1, ATT_TQ, NSA_WIDTH), lambda b, i: (b, i, 0)),
            out_shape=jax.ShapeDtypeStruct((B, T, NSA_WIDTH), F32),
            compiler_params=_cparams("parallel", "parallel"),
            name="nsa_attn_fixed" if fixed_max else "nsa_attn_online",
        )

    args = (qt, gates, kc, vct, ks, vst, kw, vwt, ovt, ogt, mb)
    return lax.cond(jnp.max(bounds) <= FIXED_MAX_LIMIT, lambda a: call(True)(*a), lambda a: call(False)(*a), args)


def _shift_rows(x, hist, k):
    xr = pltpu.roll(x, k, 0)
    hr = pltpu.roll(hist, k, 0)
    row = lax.broadcasted_iota(jnp.int32, hist.shape, 0)
    top = jnp.where(row < k, hr, xr[:SUBLANES])
    return jnp.concatenate([top, xr[SUBLANES:]], axis=0)


def _causal_conv(x, hist, w_ref, width):
    y = x * w_ref[width - 1:width, :]
    for k in range(1, width):
        y = y + _shift_rows(x, hist, k) * w_ref[width - 1 - k:width - k, :]
    return y


def _gdn_prep_kernel(x_ref, sm_ref, w_ref, alog_ref, dtb_ref,
                     q_ref, k_ref, v_ref, sg_ref, gl_ref, hist_ref):
    @pl.when(pl.program_id(1) == 0)
    def _():
        hist_ref[...] = jnp.zeros_like(hist_ref)

    x = x_ref[0]
    y = _silu(_causal_conv(x, hist_ref[...], w_ref, GDN_CONV))
    hist_ref[...] = x[x.shape[0] - SUBLANES:, :]
    for h in range(GDN_HEADS):
        for src, dst, scale in ((0, q_ref, GDN_HEAD_DIM ** -0.5), (1, k_ref, None)):
            c0 = src * GDN_WIDTH + h * GDN_HEAD_DIM
            t = y[:, c0:c0 + GDN_HEAD_DIM]
            t = t * lax.rsqrt(jnp.sum(t * t, axis=-1, keepdims=True) + NORM_EPS)
            if scale is not None:
                t = t * scale
            dst[0, :, h * GDN_HEAD_DIM:(h + 1) * GDN_HEAD_DIM] = t
    v_ref[0] = y[:, 2 * GDN_WIDTH:]
    sm = sm_ref[0]
    sg_ref[0] = _sigmoid(sm)
    a = sm + dtb_ref[...]
    softplus = jnp.maximum(a, 0.0) + jnp.log(1.0 + jnp.exp(-jnp.abs(a)))
    gl_ref[0] = -jnp.exp(alog_ref[...]) * softplus


def _gdn_prep(gqkv, small, conv_w, a_log, dt_bias, B, T, tm=512):
    tm = min(tm, T)
    w = jnp.concatenate([conv_w, jnp.zeros((SUBLANES - GDN_CONV, 3 * GDN_WIDTH), F32)], axis=0)
    place = lambda v: jnp.zeros((1, SMALL_WIDTH), F32).at[0, GDN_A_COL:GDN_A_COL + GDN_HEADS].set(v)
    tok = lambda c: pl.BlockSpec((1, tm, c), lambda b, i: (b, i, 0))
    return pl.pallas_call(
        _gdn_prep_kernel,
        grid=(B, T // tm),
        in_specs=[tok(3 * GDN_WIDTH), tok(SMALL_WIDTH), _resident((SUBLANES, 3 * GDN_WIDTH)),
                  _resident((1, SMALL_WIDTH)), _resident((1, SMALL_WIDTH))],
        out_specs=[tok(GDN_WIDTH), tok(GDN_WIDTH), tok(GDN_WIDTH), tok(SMALL_WIDTH), tok(SMALL_WIDTH)],
        out_shape=[jax.ShapeDtypeStruct((B, T, GDN_WIDTH), F32)] * 3
                  + [jax.ShapeDtypeStruct((B, T, SMALL_WIDTH), F32)] * 2,
        scratch_shapes=[pltpu.VMEM((SUBLANES, 3 * GDN_WIDTH), F32)],
        compiler_params=_cparams("parallel", "arbitrary"),
        name="gdn_prep",
    )(gqkv.reshape(B, T, 3 * GDN_WIDTH), small.reshape(B, T, SMALL_WIDTH), w, place(a_log), place(dt_bias))


GDN_NC = 8
GDN_ROWS = 2


def _bmm(a, b):
    return jnp.einsum('cij,cjk->cik', a.astype(BF16), b.astype(BF16), preferred_element_type=F32)


def _bmm_nt(a, b):
    return jnp.einsum('cid,cjd->cij', a.astype(BF16), b.astype(BF16), preferred_element_type=F32)


def _unit_lower_inverse(a, ii, jj):
    eye = (ii == jj).astype(F32)
    blk = lambda n: (ii // n) == (jj // n)
    ad = jnp.where(blk(16), a, 0.0)
    t = eye - ad
    p = ad
    for _ in range(3):
        p = _bmm(p, p)
        t = _bmm(t, eye + p)
    for n in (32, 64):
        off = jnp.where(blk(n) & ~blk(n // 2), a, 0.0)
        t = t - _bmm(_bmm(t, off), t)
    return t


def _gdn_scan_kernel(q_ref, k_ref, v_ref, sg_ref, gl_ref, z_ref, og_ref, o_ref,
                     s_ref, u_scr, wq_scr, akd_scr, egl_scr, o_scr):
    C, DK, NC, H, R = GDN_CHUNK, GDN_HEAD_DIM, GDN_NC, GDN_HEADS, GDN_ROWS
    BN = R * NC

    @pl.when(pl.program_id(1) == 0)
    def _():
        s_ref[...] = jnp.zeros_like(s_ref)

    ii = lax.broadcasted_iota(jnp.int32, (BN, C, C), 1)
    jj = lax.broadcasted_iota(jnp.int32, (BN, C, C), 2)
    causal = ii >= jj
    sg = sg_ref[...]
    gl = gl_ref[...]

    for h in range(H):
        sl = slice(h * DK, (h + 1) * DK)
        q3 = q_ref[:, :, sl].reshape(BN, C, DK)
        k3 = k_ref[:, :, sl].reshape(BN, C, DK)
        v3 = v_ref[:, :, sl].reshape(BN, C, DK)
        g_col = gl[:, :, GDN_A_COL + h:GDN_A_COL + h + 1].reshape(BN, C, 1)
        b_col = sg[:, :, GDN_B_COL + h:GDN_B_COL + h + 1].reshape(BN, C, 1)
        g_mat = jnp.broadcast_to(g_col, (BN, C, C))
        g_row = jnp.sum(jnp.where(ii == jj, g_mat, 0.0), axis=1, keepdims=True)
        gc_col = jnp.sum(jnp.where(causal, jnp.broadcast_to(g_row, (BN, C, C)), 0.0), axis=2, keepdims=True)
        gc_row = jnp.sum(jnp.where(ii <= jj, g_mat, 0.0), axis=1, keepdims=True)
        decay = jnp.where(causal, jnp.exp(jnp.where(causal, gc_col - gc_row, 0.0)), 0.0)
        kb = k3 * b_col
        kq = _bmm_nt(jnp.concatenate([kb, q3], axis=1), k3)
        a_kk = jnp.where(ii > jj, kq[:, :C] * decay, 0.0)
        a_qk = kq[:, C:] * decay
        t_inv = _unit_lower_inverse(a_kk, ii, jj)
        egc = jnp.exp(gc_col)
        uw = _bmm(t_inv, jnp.concatenate([v3 * b_col, kb * egc], axis=2))
        g_last = gc_col[:, C - 1:C, :]
        kd = k3 * jnp.exp(g_last - gc_col)
        u_scr[h] = uw[:, :, :DK]
        wq_scr[h] = jnp.concatenate([uw[:, :, DK:], q3 * egc], axis=1).astype(BF16)
        akd_scr[h] = jnp.concatenate([a_qk, jnp.swapaxes(kd, 1, 2)], axis=1).astype(BF16)
        egl_scr[h] = jnp.broadcast_to(jnp.exp(g_last), (BN, SUBLANES, DK))

    def step(c, carry):
        for row in range(R):
            n = row * NC + c
            r0 = pl.multiple_of(n * C, C)
            for h in range(H):
                s = s_ref[row * H + h]
                r = jnp.dot(wq_scr[h, n], s.astype(BF16), preferred_element_type=F32)
                v_new = (u_scr[h, n] - r[:C]).astype(BF16)
                r2 = jnp.dot(akd_scr[h, n], v_new, preferred_element_type=F32)
                o_scr[pl.ds(r0, C), h * DK:(h + 1) * DK] = r[C:] + r2[:C]
                s_ref[row * H + h] = s * egl_scr[h, n][0:1, :] + r2[C:]
        return carry

    lax.fori_loop(0, NC, step, 0)

    for h in range(H):
        sl = slice(h * DK, (h + 1) * DK)
        o = o_scr[:, sl]
        o = o * lax.rsqrt(jnp.mean(o * o, axis=-1, keepdims=True) + NORM_EPS) * og_ref[...]
        o_ref[:, :, sl] = o.reshape(R, NC * C, DK) * _silu(z_ref[:, :, sl])


def _gdn_scan(q, k, v, sg, gl, z, out_gain, B, T):
    C, NC, H, DK, R = GDN_CHUNK, GDN_NC, GDN_HEADS, GDN_HEAD_DIM, GDN_ROWS
    ts = NC * C
    assert B % R == 0 and T % ts == 0
    tok = lambda c: pl.BlockSpec((R, ts, c), lambda b, i: (b, i, 0))
    return pl.pallas_call(
        _gdn_scan_kernel,
        grid=(B // R, T // ts),
        in_specs=[tok(GDN_WIDTH), tok(GDN_WIDTH), tok(GDN_WIDTH), tok(SMALL_WIDTH), tok(SMALL_WIDTH),
                  tok(GDN_WIDTH), _resident((1, GDN_HEAD_DIM))],
        out_specs=tok(GDN_WIDTH),
        out_shape=jax.ShapeDtypeStruct((B, T, GDN_WIDTH), F32),
        scratch_shapes=[pltpu.VMEM((R * H, DK, DK), F32),
                        pltpu.VMEM((H, R * NC, C, DK), F32),
                        pltpu.VMEM((H, R * NC, 2 * C, DK), BF16),
                        pltpu.VMEM((H, R * NC, C + DK, C), BF16),
                        pltpu.VMEM((H, R * NC, SUBLANES, DK), F32),
                        pltpu.VMEM((R * ts, GDN_WIDTH), F32)],
        compiler_params=_cparams("parallel", "arbitrary"),
        name="gdn_scan",
    )(q, k, v, sg, gl, z.reshape(B, T, GDN_WIDTH), out_gain.reshape(1, GDN_HEAD_DIM))


def _out_proj_kernel(x_ref, a_ref, b_ref, w_ref, o_ref):
    acc = jnp.dot(a_ref[...].astype(BF16), w_ref[0, :NSA_WIDTH, :], preferred_element_type=F32)
    acc = acc + jnp.dot(b_ref[...].astype(BF16), w_ref[0, NSA_WIDTH:, :], preferred_element_type=F32)
    o_ref[...] = x_ref[...] + acc


def _out_proj(x2, o_nsa, o_gdn, w_all, layer, tm=512):
    m = x2.shape[0]
    tm = min(tm, m)
    row = lambda c: pl.BlockSpec((tm, c), lambda i: (i, 0))
    return pl.pallas_call(
        _out_proj_kernel,
        grid=(m // tm,),
        in_specs=[row(D_MODEL), row(NSA_WIDTH), row(GDN_WIDTH),
                  pl.BlockSpec((1, NSA_WIDTH + GDN_WIDTH, D_MODEL), lambda i: (layer, 0, 0),
                               pipeline_mode=pl.Buffered(1))],
        out_specs=row(D_MODEL),
        out_shape=jax.ShapeDtypeStruct((m, D_MODEL), F32),
        compiler_params=_cparams("parallel"),
        name="out_proj",
    )(x2, o_nsa.reshape(m, NSA_WIDTH), o_gdn.reshape(m, GDN_WIDTH), w_all)


FFN_CHUNK = 256


FFN_HEAD_ROWS = 16


def _ffn_kernel(x_ref, g_ref, wu_ref, cw_ref, cb_ref, wd_ref, o_ref, hist_ref, act_ref):
    @pl.when(pl.program_id(1) == 0)
    def _():
        hist_ref[...] = jnp.zeros_like(hist_ref)

    x = x_ref[0]
    tm = x.shape[0]
    h = (x * lax.rsqrt(jnp.mean(x * x, axis=-1, keepdims=True) + NORM_EPS) * g_ref[...]).astype(BF16)

    def conv_cols(c0):
        cols = slice(c0, c0 + FFN_CHUNK)
        u = jnp.dot(h, wu_ref[0, :, cols], preferred_element_type=F32)
        w = [cw_ref[0, j:j + 1, cols] for j in range(FFN_CONV)]
        b = cb_ref[0, :, cols]
        y = u * w[FFN_CONV - 1] + b
        head = u[:FFN_HEAD_ROWS]
        y_head = head * w[FFN_CONV - 1] + b
        hist = hist_ref[:, cols]
        for k in range(1, FFN_CONV):
            y = y + pltpu.roll(u, k, 0) * w[FFN_CONV - 1 - k]
            y_head = y_head + _shift_rows(head, hist, k) * w[FFN_CONV - 1 - k]
        hist_ref[:, cols] = u[tm - SUBLANES:, :]
        return y, y_head

    for c in range(D_FF // FFN_CHUNK):
        cols = slice(c * FFN_CHUNK, (c + 1) * FFN_CHUNK)
        gate, gate_head = conv_cols(c * FFN_CHUNK)
        up, up_head = conv_cols(D_FF + c * FFN_CHUNK)
        act_ref[:, cols] = (_silu(gate) * up).astype(BF16)
        act_ref[:FFN_HEAD_ROWS, cols] = (_silu(gate_head) * up_head).astype(BF16)
    o_ref[0] = x + jnp.dot(act_ref[...], wd_ref[0], preferred_element_type=F32)


def _ffn(x3, gain, wu_all, cw_all, cb_all, wd_all, layer, tm=512):
    B, T, _ = x3.shape
    tm = min(tm, T)
    lay = lambda s1, s2: pl.BlockSpec((1, s1, s2), lambda b, i: (layer, 0, 0), pipeline_mode=pl.Buffered(1))
    return pl.pallas_call(
        _ffn_kernel,
        grid=(B, T // tm),
        in_specs=[pl.BlockSpec((1, tm, D_MODEL), lambda b, i: (b, i, 0)),
                  _resident((1, D_MODEL)),
                  lay(D_MODEL, 2 * D_FF), lay(SUBLANES, 2 * D_FF), lay(1, 2 * D_FF), lay(D_FF, D_MODEL)],
        out_specs=pl.BlockSpec((1, tm, D_MODEL), lambda b, i: (b, i, 0)),
        out_shape=jax.ShapeDtypeStruct((B, T, D_MODEL), F32),
        scratch_shapes=[pltpu.VMEM((SUBLANES, 2 * D_FF), F32),
                        pltpu.VMEM((tm, D_FF), BF16)],
        compiler_params=_cparams("parallel", "arbitrary"),
        name="ffn",
    )(x3, gain.reshape(1, D_MODEL), wu_all, cw_all, cb_all, wd_all)


def kernel(x, positions, attn_norm, w_in, nsa_q_norm, nsa_k_norm, cmp_pe, cmp_w1, cmp_w2, nsa_out_norm,
           gdn_conv_w, gdn_a_log, gdn_dt_bias, gdn_out_norm, w_out, ffn_norm, w_up, ffn_conv_w,
           ffn_conv_b, w_down):
    B, T, _ = x.shape
    depth = w_in.shape[0]
    M = B * T
    tabs = _rope_tables(positions)
    w_in_r = _regroup_w_in(w_in)
    pe, w1g, w2g = _compress_weights(cmp_pe, cmp_w1, cmp_w2)
    w_out_b = w_out.astype(BF16)
    w_up_b = w_up.astype(BF16)
    w_down_b = w_down.astype(BF16)
    cw = jnp.concatenate([ffn_conv_w, jnp.zeros((depth, SUBLANES - FFN_CONV, 2 * D_FF), F32)], axis=1)
    cb = ffn_conv_b.reshape(depth, 1, 2 * D_FF)

    for l in range(depth):
        x2 = x.reshape(M, D_MODEL)
        q_raw, kv_raw, gqkv, z, small = _in_proj(x2, attn_norm[l], w_in_r, l)
        qt, kc_in, ks, vst, kw, vwt = _nsa_prep(q_raw, kv_raw, tabs, nsa_q_norm[l], nsa_k_norm[l], B, T)
        cmp_t, cmp_n = _compress(kc_in, kv_raw.reshape(B, T, NSA_KV_WIDTH), pe, w1g, w2g,
                                 nsa_k_norm[l, 0], l, B, T)
        gq, gk, gv, sg, gl = _gdn_prep(gqkv, small, gdn_conv_w[l], gdn_a_log[l], gdn_dt_bias[l], B, T)
        o_nsa = _nsa_attn(qt, sg, cmp_n[:B], cmp_t[B:], ks, vst, kw, vwt, nsa_out_norm[l],
                          nsa_q_norm[l], nsa_k_norm[l], B, T)
        o_gdn = _gdn_scan(gq, gk, gv, sg, gl, z, gdn_out_norm[l], B, T)
        x2 = _out_proj(x2, o_nsa, o_gdn, w_out_b, l)
        x = _ffn(x2.reshape(B, T, D_MODEL), ffn_norm[l], w_up_b, cw, cb, w_down_b, l)
    return x
```

```python
import functools

import numpy as np
import jax
import jax.numpy as jnp
from jax import lax
from jax.experimental import pallas as pl
from jax.experimental.pallas import tpu as pltpu

D_MODEL = 1024
NSA_HEADS = 8
NSA_KV_GROUPS = 2
NSA_HPG = NSA_HEADS // NSA_KV_GROUPS
NSA_HEAD_DIM = 64
N_BRANCH = 3
CMP_BLOCK = 32
CMP_STRIDE = 16
CMP_HIDDEN = 256
SLC_BLOCK = 64
SLC_TOP_N = 16
WINDOW = 512
ROPE_THETA = 500000.0
ROPE_DIM = NSA_HEAD_DIM // 4
GDN_HEADS = 4
GDN_HEAD_DIM = 128
GDN_CONV = 4
GDN_CHUNK = 64
NSA_WIDTH = NSA_HEADS * NSA_HEAD_DIM
GDN_WIDTH = GDN_HEADS * GDN_HEAD_DIM
NSA_KV_WIDTH = N_BRANCH * 2 * NSA_KV_GROUPS * NSA_HEAD_DIM
NSA_GATE_WIDTH = NSA_HEADS * N_BRANCH
D_FF = 2816
FFN_CONV = 3
NORM_EPS = 1e-6

LANES = 128
SUBLANES = 8
SMALL_WIDTH = LANES
GDN_A_COL = NSA_GATE_WIDTH
GDN_B_COL = NSA_GATE_WIDTH + GDN_HEADS
VMEM_LIMIT = 56 * 1024 * 1024
NEG = -1e30
LOG2E = 1.4426950408889634

F32 = jnp.float32
BF16 = jnp.bfloat16


def _cparams(*sem):
    return pltpu.CompilerParams(dimension_semantics=sem, vmem_limit_bytes=VMEM_LIMIT)


def _resident(shape):
    nd = len(shape)
    return pl.BlockSpec(shape, lambda *_: (0,) * nd, pipeline_mode=pl.Buffered(1))


def _silu(x):
    return x * (1.0 / (1.0 + jnp.exp(-x)))


def _sigmoid(x):
    return 1.0 / (1.0 + jnp.exp(-x))


def _rope_kernel(pos_ref, inv_ref, c_ref, s1_ref, s2_ref):
    ang = pos_ref[...].astype(F32) * inv_ref[...]
    lane = lax.broadcasted_iota(jnp.int32, ang.shape, 1) % NSA_HEAD_DIM
    half = ROPE_DIM // 2
    cos = jnp.cos(ang)
    sin = jnp.sin(ang)
    c_ref[...] = jnp.where(lane < ROPE_DIM, cos, 1.0)
    s1_ref[...] = jnp.where((lane >= half) & (lane < ROPE_DIM), sin, 0.0)
    s2_ref[...] = jnp.where(lane < half, -sin, 0.0)


def _rope_tables(positions):
    m = positions.size
    tm = min(m, 1024)
    inv = ROPE_THETA ** (-jnp.arange(0, ROPE_DIM, 2, dtype=F32) / ROPE_DIM)
    lane = np.arange(LANES) % NSA_HEAD_DIM
    inv_row = jnp.where(lane < ROPE_DIM, inv[lane % (ROPE_DIM // 2)], 0.0).reshape(1, LANES)
    tab = jax.ShapeDtypeStruct((m, LANES), F32)
    return pl.pallas_call(
        _rope_kernel,
        grid=(m // tm,),
        in_specs=[pl.BlockSpec((tm, 1), lambda i: (i, 0)), _resident((1, LANES))],
        out_specs=[pl.BlockSpec((tm, LANES), lambda i: (i, 0))] * 3,
        out_shape=[tab] * 3,
        compiler_params=_cparams("parallel"),
        name="rope_tables",
    )(positions.reshape(m, 1), inv_row)


IN_SEG = (NSA_WIDTH, NSA_KV_WIDTH, 3 * GDN_WIDTH, GDN_WIDTH, SMALL_WIDTH)


def _in_proj_kernel(x_ref, g_ref, w_ref, q_ref, kv_ref, gq_ref, z_ref, sm_ref):
    x = x_ref[...]
    h = x * lax.rsqrt(jnp.mean(x * x, axis=-1, keepdims=True) + NORM_EPS) * g_ref[...]
    h = h.astype(BF16)
    off = 0
    for ref, width in zip((q_ref, kv_ref, gq_ref, z_ref, sm_ref), IN_SEG):
        ref[...] = jnp.dot(h, w_ref[0, :, off:off + width], preferred_element_type=F32)
        off += width


def _regroup_w_in(w_in):
    o_q = 0
    o_kv = o_q + NSA_WIDTH
    o_g = o_kv + NSA_KV_WIDTH
    o_gq = o_g + NSA_GATE_WIDTH
    o_a = o_gq + 3 * GDN_WIDTH
    o_b = o_a + GDN_HEADS
    o_z = o_b + GDN_HEADS
    pad = SMALL_WIDTH - NSA_GATE_WIDTH - 2 * GDN_HEADS
    parts = [w_in[..., o_q:o_g], w_in[..., o_gq:o_a], w_in[..., o_z:o_z + GDN_WIDTH],
             w_in[..., o_g:o_gq], w_in[..., o_a:o_z],
             jnp.zeros(w_in.shape[:-1] + (pad,), w_in.dtype)]
    return jnp.concatenate(parts, axis=-1).astype(BF16)


def _in_proj(x2, gain, w_all, layer, tm=512):
    m = x2.shape[0]
    tm = min(tm, m)
    npad = sum(IN_SEG)
    outs = [jax.ShapeDtypeStruct((m, w), F32) for w in IN_SEG]
    return pl.pallas_call(
        _in_proj_kernel,
        grid=(m // tm,),
        in_specs=[pl.BlockSpec((tm, D_MODEL), lambda i: (i, 0)),
                  _resident((1, D_MODEL)),
                  pl.BlockSpec((1, D_MODEL, npad), lambda i: (layer, 0, 0), pipeline_mode=pl.Buffered(1))],
        out_specs=[pl.BlockSpec((tm, w), lambda i: (i, 0)) for w in IN_SEG],
        out_shape=outs,
        compiler_params=_cparams("parallel"),
        name="in_proj",
    )(x2, gain.reshape(1, D_MODEL), w_all)


def _head_rms(x, gain_row):
    lane = lax.broadcasted_iota(jnp.int32, x.shape, 1)
    x2 = x * x
    left = jnp.sum(jnp.where(lane < NSA_HEAD_DIM, x2, 0.0), axis=-1, keepdims=True)
    right = jnp.sum(jnp.where(lane >= NSA_HEAD_DIM, x2, 0.0), axis=-1, keepdims=True)
    ms = jnp.where(lane < NSA_HEAD_DIM, left, right) * (1.0 / NSA_HEAD_DIM)
    return x * lax.rsqrt(ms + NORM_EPS) * gain_row


def _rope(x, c, s1, s2):
    half = ROPE_DIM // 2
    return x * c + pltpu.roll(x, half, 1) * s1 + pltpu.roll(x, LANES - half, 1) * s2


V_ROWS = NSA_HEAD_DIM + 16


def _values_with_ones(v):
    vt = v.T
    tm = v.shape[0]
    extra = jnp.where(lax.broadcasted_iota(jnp.int32, (V_ROWS - NSA_HEAD_DIM, tm), 0) == 0, 1.0, 0.0)
    parts = []
    for g in range(NSA_KV_GROUPS):
        parts += [vt[g * NSA_HEAD_DIM:(g + 1) * NSA_HEAD_DIM], extra]
    return jnp.concatenate(parts, axis=0).astype(BF16)


def _nsa_prep_kernel(q_ref, kv_ref, c_ref, s1_ref, s2_ref, qg_ref, kg_ref,
                     qt_ref, kc_ref, ks_ref, vst_ref, kw_ref, vwt_ref):
    c, s1, s2 = c_ref[...], s1_ref[...], s2_ref[...]
    qg = qg_ref[...]
    tm = c.shape[0]
    for j in range(NSA_WIDTH // LANES):
        x = q_ref[0, :, j * LANES:(j + 1) * LANES]
        y = _rope(_head_rms(x, qg), c, s1, s2) * (NSA_HEAD_DIM ** -0.5) * LOG2E
        qt_ref[0, j * LANES:(j + 1) * LANES, :] = y.T.astype(BF16)
    kv = lambda i: kv_ref[0, :, i * LANES:(i + 1) * LANES]
    kc_ref[0] = _rope(kv(0), c, s1, s2)
    ks = _rope(_head_rms(kv(2), kg_ref[1:2, :]), c, s1, s2)
    lane = lax.broadcasted_iota(jnp.int32, (tm, LANES), 1)
    blk = (pl.program_id(1) * tm + lax.broadcasted_iota(jnp.int32, (tm, LANES), 0)) // SLC_BLOCK
    onehot = jnp.where(lane - NSA_HEAD_DIM == blk, 1.0, 0.0)
    ks_ref[0, :, :LANES] = jnp.where(lane < NSA_HEAD_DIM, ks, onehot).astype(BF16)
    ks_ref[0, :, LANES:] = jnp.where(lane < NSA_HEAD_DIM, pltpu.roll(ks, NSA_HEAD_DIM, 1), onehot).astype(BF16)
    vst_ref[0] = _values_with_ones(kv(3))
    kw_ref[0] = _rope(_head_rms(kv(4), kg_ref[2:3, :]), c, s1, s2).astype(BF16)
    vwt_ref[0] = _values_with_ones(kv(5))


def _nsa_prep(q_raw, kv_raw, tabs, q_gain, k_gain, B, T, tm=512):
    tm = min(tm, T)
    nt = T // tm
    c, s1, s2 = tabs
    qg = jnp.tile(q_gain.reshape(1, NSA_HEAD_DIM), (1, 2))
    kg = jnp.tile(k_gain, (1, 2))
    kg = jnp.concatenate([kg, jnp.zeros((5, LANES), F32)], axis=0)
    tok = lambda w: pl.BlockSpec((1, tm, w), lambda b, i: (b, i, 0))
    tab = pl.BlockSpec((tm, LANES), lambda b, i: (b * nt + i, 0))
    tr = pl.BlockSpec((1, NSA_KV_GROUPS * V_ROWS, tm), lambda b, i: (b, 0, i))
    return pl.pallas_call(
        _nsa_prep_kernel,
        grid=(B, nt),
        in_specs=[tok(NSA_WIDTH), tok(NSA_KV_WIDTH), tab, tab, tab,
                  _resident((1, LANES)), _resident((8, LANES))],
        out_specs=[pl.BlockSpec((1, NSA_WIDTH, tm), lambda b, i: (b, 0, i)),
                   tok(LANES), tok(2 * LANES), tr, tok(LANES), tr],
        out_shape=[jax.ShapeDtypeStruct((B, NSA_WIDTH, T), BF16),
                   jax.ShapeDtypeStruct((B, T, LANES), F32),
                   jax.ShapeDtypeStruct((B, T, 2 * LANES), BF16),
                   jax.ShapeDtypeStruct((B, NSA_KV_GROUPS * V_ROWS, T), BF16),
                   jax.ShapeDtypeStruct((B, T, LANES), BF16),
                   jax.ShapeDtypeStruct((B, NSA_KV_GROUPS * V_ROWS, T), BF16)],
        compiler_params=_cparams("parallel", "parallel"),
        name="nsa_prep",
    )(q_raw.reshape(B, T, NSA_WIDTH), kv_raw.reshape(B, T, NSA_KV_WIDTH), c, s1, s2, qg, kg)


CMP_ROW = CMP_STRIDE * LANES
CMP_HID2 = NSA_KV_GROUPS * CMP_HIDDEN


def _compress_kernel(r_ref, pe_ref, w1_ref, w2_ref, kg_ref, t_ref, n_ref):
    is_k = pl.program_id(0) == 0
    r = r_ref[0, 0]
    ha = jnp.dot((r + pe_ref[0, 0:1, :]).astype(BF16), w1_ref[0, 0], preferred_element_type=F32)
    hb = jnp.dot((r + pe_ref[0, 1:2, :]).astype(BF16), w1_ref[0, 1], preferred_element_type=F32)
    n16 = r.shape[0]
    hid = ha + pltpu.roll(hb, n16 - 1, 0)
    y = jnp.dot(_silu(hid).astype(BF16), w2_ref[0], preferred_element_type=F32)
    y = jnp.where(is_k, _head_rms(y, kg_ref[...]), y)
    t_ref[0] = y.T.astype(BF16)
    n_ref[0] = y.astype(BF16)


def _compress_weights(cmp_pe, cmp_w1, cmp_w2):
    L = cmp_w1.shape[0]
    G, DH, H = NSA_KV_GROUPS, NSA_HEAD_DIM, CMP_HIDDEN
    w1 = cmp_w1.reshape(L, 2, 2, CMP_STRIDE, DH, H)
    eye = jnp.eye(G, dtype=cmp_w1.dtype)
    w1g = jnp.einsum('lkspdh,gf->lkspgdfh', w1, eye)
    w1g = w1g.reshape(L, 2, 2, CMP_ROW, G * H).astype(BF16)
    w2g = jnp.einsum('lkhd,gf->lkghfd', cmp_w2, eye).reshape(L, 2, G * H, G * DH).astype(BF16)
    pe = cmp_pe.reshape(L, 2, 2, CMP_STRIDE, 1, DH)
    pe = jnp.broadcast_to(pe, (L, 2, 2, CMP_STRIDE, G, DH)).reshape(L, 2, 2, CMP_ROW)
    return pe, w1g, w2g


def _compress(kc_in, kv_raw3, pe, w1g, w2g, k_gain0, layer, B, T):
    n16 = T // CMP_STRIDE
    rk = kc_in.reshape(B, n16, CMP_ROW)
    rv = kv_raw3[:, :, LANES:2 * LANES].reshape(B, n16, CMP_ROW)
    r = jnp.stack([rk, rv], axis=0)
    kg = jnp.tile(k_gain0.reshape(1, NSA_HEAD_DIM), (1, 2))
    return pl.pallas_call(
        _compress_kernel,
        grid=(2, B),
        in_specs=[pl.BlockSpec((1, 1, n16, CMP_ROW), lambda s, b: (s, b, 0, 0)),
                  pl.BlockSpec((1, 2, CMP_ROW), lambda s, b: (layer * 2 + s, 0, 0)),
                  pl.BlockSpec((1, 2, CMP_ROW, CMP_HID2), lambda s, b: (layer * 2 + s, 0, 0, 0)),
                  pl.BlockSpec((1, CMP_HID2, LANES), lambda s, b: (layer * 2 + s, 0, 0)),
                  _resident((1, LANES))],
        out_specs=[pl.BlockSpec((1, LANES, n16), lambda s, b: (s * B + b, 0, 0)),
                   pl.BlockSpec((1, n16, LANES), lambda s, b: (s * B + b, 0, 0))],
        out_shape=[jax.ShapeDtypeStruct((2 * B, LANES, n16), BF16),
                   jax.ShapeDtypeStruct((2 * B, n16, LANES), BF16)],
        compiler_params=_cparams("arbitrary", "parallel"),
        name="compress",
    )(r, pe.reshape(-1, 2, CMP_ROW), w1g.reshape(-1, 2, CMP_ROW, CMP_HID2),
      w2g.reshape(-1, CMP_HID2, LANES), kg)


ATT_TQ = 256
ATT_TK = 512
WIN_KEYS = WINDOW + ATT_TQ
SLC_MASK_DIM = 64


def _softmax_cols(s):
    m = jnp.max(s, axis=0, keepdims=True)
    p = jnp.where(m > 0.5 * NEG, jnp.exp2(s - m), 0.0)
    return p, 1.0 / jnp.maximum(jnp.sum(p, axis=0, keepdims=True), 1e-30)


def _flash_update(s, m, acc, v_ones, fixed_max):
    if fixed_max:
        p = jnp.exp2(s - m).astype(BF16)
        return m, acc + jnp.dot(v_ones, p, preferred_element_type=F32)
    m_new = jnp.maximum(m, jnp.max(s, axis=0, keepdims=True))
    p = jnp.exp2(s - m_new).astype(BF16)
    acc = acc * jnp.exp2(m - m_new) + jnp.dot(v_ones, p, preferred_element_type=F32)
    return m_new, acc


def _flash_finish(acc):
    dh = NSA_HEAD_DIM
    return acc[:dh] * (1.0 / jnp.maximum(acc[dh:dh + 1], 1e-30))


def _nsa_attn_kernel(qt_ref, gt_ref, kc_ref, vct_ref, ks_ref, vst_ref, kw_ref, vwt_ref,
                     ovt_ref, ogt_ref, mb_ref, o_ref, *, n_cmp_pad, fixed_max):
    TQ, TK, H, DH, NB = ATT_TQ, ATT_TK, NSA_HPG, NSA_HEAD_DIM, SLC_MASK_DIM
    t0 = pl.program_id(1) * TQ
    heads = lambda a: jnp.concatenate([a] * H, axis=1)
    t_row = t0 + lax.broadcasted_iota(jnp.int32, (1, TQ), 1)

    ci = lax.broadcasted_iota(jnp.int32, (n_cmp_pad, TQ), 0)
    bias_c = heads(jnp.where(ci * CMP_STRIDE + (CMP_BLOCK - 1) <= t_row, 0.0, NEG))
    w0 = pl.multiple_of(jnp.maximum(t0 - WINDOW, 0), LANES)
    rel = t_row - (w0 + lax.broadcasted_iota(jnp.int32, (WIN_KEYS, TQ), 0))
    bias_w = heads(jnp.where((rel >= 0) & (rel < WINDOW), 0.0, NEG))
    kd = t0 // TK
    k0d = pl.multiple_of(kd * TK, TK)
    bias_d = heads(jnp.where(k0d + lax.broadcasted_iota(jnp.int32, (TK, TQ), 0) <= t_row, 0.0, NEG))

    jj = lax.broadcasted_iota(jnp.int32, (NB, TQ), 0)
    cur = jnp.broadcast_to(t_row // SLC_BLOCK, (NB, TQ))
    forced = (jj == 0) | (jj == cur) | (jj == cur - 1)
    j8 = lax.broadcasted_iota(jnp.int32, (SUBLANES, TQ), 0)

    sg_t = gt_ref[0].T
    zeros_q = jnp.zeros((DH, TQ), BF16)
    G = NSA_KV_GROUPS
    rhs_q, rhs_s, o_c = [], [], []

    for g in range(G):
        lo, hi = g * DH, (g + 1) * DH
        q_t = [qt_ref[0, (g * H + h) * DH:(g * H + h + 1) * DH, :] for h in range(H)]
        pad = (lambda x: jnp.concatenate([x, zeros_q], axis=0)) if g == 0 else \
              (lambda x: jnp.concatenate([zeros_q, x], axis=0))
        rhs_q.append(jnp.concatenate([pad(x) for x in q_t], axis=1))

        pc, inv_c = _softmax_cols(jnp.dot(kc_ref[0], rhs_q[g], preferred_element_type=F32) + bias_c)
        o_c.append(jnp.dot(vct_ref[0, lo:hi, :], pc.astype(BF16), preferred_element_type=F32) * inv_c)

        def ranked_mask(pc=pc, inv_c=inv_c):
            psum = pc[:, :TQ] * inv_c[:, :TQ]
            for h in range(1, H):
                psum = psum + pc[:, h * TQ:(h + 1) * TQ] * inv_c[:, h * TQ:(h + 1) * TQ]
            imp = jnp.dot(ovt_ref[...], psum, preferred_element_type=F32, precision=lax.Precision.HIGHEST)
            val = jnp.where(forced, jnp.inf, jnp.where(jj > cur, -jnp.inf, imp))
            vals = [val[SUBLANES * v:SUBLANES * (v + 1)] for v in range(NB // SUBLANES)]
            cnts = [jnp.zeros((SUBLANES, TQ), F32) for _ in vals]
            for jp in range(NB):
                row = val[jp:jp + 1, :]
                for v in range(len(vals)):
                    if SUBLANES * v > jp:
                        beat = jnp.where(row >= vals[v], 1.0, 0.0)
                    elif SUBLANES * v + SUBLANES - 1 < jp:
                        beat = jnp.where(row > vals[v], 1.0, 0.0)
                    else:
                        beat = jnp.where(j8 > jp - SUBLANES * v, jnp.where(row >= vals[v], 1.0, 0.0),
                                         jnp.where(row > vals[v], 1.0, 0.0))
                    cnts[v] = cnts[v] + beat
            cnt = jnp.concatenate(cnts, axis=0)
            return jnp.where((cnt < float(SLC_TOP_N)) & (jj <= cur), 0.0, NEG).astype(BF16)

        def causal_mask():
            return jnp.where(jj <= cur, 0.0, NEG).astype(BF16)

        mask_t = lax.cond(t0 + TQ > SLC_TOP_N * SLC_BLOCK, ranked_mask, causal_mask)

        rhs_s.append(jnp.concatenate([jnp.concatenate([x, mask_t], axis=0) for x in q_t], axis=1))

    def slc_tile(k0, carry, bias):
        out = []
        for g in range(G):
            s = jnp.dot(ks_ref[0, pl.ds(k0, TK), g * LANES:(g + 1) * LANES], rhs_s[g], preferred_element_type=F32)
            if bias is not None:
                s = s + bias
            out.append(_flash_update(s, *carry[g], vst_ref[0, g * V_ROWS:(g + 1) * V_ROWS, pl.ds(k0, TK)],
                                     fixed_max))
        return tuple(out)

    acc0 = jnp.zeros((V_ROWS, H * TQ), F32)
    m0_s = mb_ref[0:1, :] if fixed_max else jnp.full((1, H * TQ), NEG, F32)
    m0_w = mb_ref[1:2, :] if fixed_max else jnp.full((1, H * TQ), NEG, F32)
    carry = lax.fori_loop(0, kd, lambda kt, c: slc_tile(pl.multiple_of(kt * TK, TK), c, None), ((m0_s, acc0),) * G)
    carry = slc_tile(k0d, carry, bias_d)

    out_rows = []
    for g in range(G):
        o_s = _flash_finish(carry[g][1])

        sw = jnp.dot(kw_ref[0, pl.ds(w0, WIN_KEYS), :], rhs_q[g], preferred_element_type=F32) + bias_w
        o_w = _flash_finish(_flash_update(sw, m0_w, acc0, vwt_ref[0, g * V_ROWS:(g + 1) * V_ROWS,
                                                                  pl.ds(w0, WIN_KEYS)], fixed_max)[1])

        def gate(br):
            rows = [(g * H + h) * N_BRANCH + br for h in range(H)]
            return jnp.concatenate([sg_t[r:r + 1, :] for r in rows], axis=1)

        o = gate(0) * o_c[g] + gate(1) * o_s + gate(2) * o_w
        o = o * lax.rsqrt(jnp.mean(o * o, axis=0, keepdims=True) + NORM_EPS)
        for h in range(H):
            out_rows.append(o[:, h * TQ:(h + 1) * TQ] * ogt_ref[:, g * H + h:g * H + h + 1])

    o_ref[0] = jnp.concatenate(out_rows, axis=0).T


def _nsa_consts(T):
    n_cmp = (T - CMP_BLOCK) // CMP_STRIDE + 1
    n_slc = T // SLC_BLOCK
    n_cmp_pad = T // CMP_STRIDE
    cs = np.arange(n_cmp_pad) * CMP_STRIDE
    ss = np.arange(SLC_MASK_DIM) * SLC_BLOCK
    ov = ((cs[None, :] < ss[:, None] + SLC_BLOCK) & (cs[None, :] + CMP_BLOCK > ss[:, None]))
    ov = ov & (np.arange(n_cmp_pad)[None, :] < n_cmp) & (np.arange(SLC_MASK_DIM)[:, None] < n_slc)
    return jnp.asarray(ov, F32), n_cmp_pad


FIXED_MAX_LIMIT = 40.0


def _score_bound(q_gain, k_gain):
    return 1.01 * LOG2E * (NSA_HEAD_DIM ** 0.5) * jnp.max(jnp.abs(q_gain)) * jnp.max(jnp.abs(k_gain))


def _nsa_attn(qt, gates, kc, vct, ks, vst, kw, vwt, out_gain, q_gain, k_gain, B, T):
    ovt, n_cmp_pad = _nsa_consts(T)
    assert T // SLC_BLOCK <= SLC_MASK_DIM and T % ATT_TK == 0 and T >= WIN_KEYS
    nq = T // ATT_TQ
    ogt = jnp.zeros((NSA_HEAD_DIM, LANES), F32).at[:, :NSA_HEADS].set(out_gain.T)
    bounds = jnp.stack([_score_bound(q_gain, k_gain[1]), _score_bound(q_gain, k_gain[2])])
    mb = jnp.zeros((SUBLANES, NSA_HPG * ATT_TQ), F32).at[:2].set(bounds[:, None])
    per_b = lambda s1, s2: pl.BlockSpec((1, s1, s2), lambda b, i: (b, 0, 0))

    def call(fixed_max):
        return pl.pallas_call(
            functools.partial(_nsa_attn_kernel, n_cmp_pad=n_cmp_pad, fixed_max=fixed_max),
            grid=(B, nq),
            in_specs=[pl.BlockSpec((1, NSA_WIDTH, ATT_TQ), lambda b, i: (b, 0, i)),
                      pl.BlockSpec((1, ATT_TQ, SMALL_WIDTH), lambda b, i: (b, i, 0)),
                      per_b(n_cmp_pad, LANES), per_b(LANES, n_cmp_pad),
                      per_b(T, 2 * LANES), per_b(NSA_KV_GROUPS * V_ROWS, T),
                      per_b(T, LANES), per_b(NSA_KV_GROUPS * V_ROWS, T),
                      _resident((SLC_MASK_DIM, n_cmp_pad)), _resident((NSA_HEAD_DIM, LANES)),
                      _resident((SUBLANES, NSA_HPG * ATT_TQ))],
            out_specs=pl.BlockSpec((1, ATT_TQ, NSA_WIDTH), lambda b, i: (b, i, 0)),
            out_shape=jax.ShapeDtypeStruct((B, T, NSA_WIDTH), F32),
            compiler_params=_cparams("parallel", "parallel"),
            name="nsa_attn_fixed" if fixed_max else "nsa_attn_online",
        )

    args = (qt, gates, kc, vct, ks, vst, kw, vwt, ovt, ogt, mb)
    return lax.cond(jnp.max(bounds) <= FIXED_MAX_LIMIT, lambda a: call(True)(*a), lambda a: call(False)(*a), args)


def _shift_rows(x, hist, k):
    xr = pltpu.roll(x, k, 0)
    hr = pltpu.roll(hist, k, 0)
    row = lax.broadcasted_iota(jnp.int32, hist.shape, 0)
    top = jnp.where(row < k, hr, xr[:SUBLANES])
    if x.shape[0] == SUBLANES:
        return top
    return jnp.concatenate([top, xr[SUBLANES:]], axis=0)


def _gdn_qkv(y):
    cols = []
    for src, scale in ((0, GDN_HEAD_DIM ** -0.5), (1, None)):
        for h in range(GDN_HEADS):
            c0 = src * GDN_WIDTH + h * GDN_HEAD_DIM
            t = _silu(y[:, c0:c0 + GDN_HEAD_DIM])
            t = t * lax.rsqrt(jnp.sum(t * t, axis=-1, keepdims=True) + NORM_EPS)
            cols.append(t if scale is None else t * scale)
    cols.append(_silu(y[:, 2 * GDN_WIDTH:]))
    return cols


GDN_NC = 8
GDN_ROWS = 2


def _bmm(a, b):
    return jnp.einsum('cij,cjk->cik', a.astype(BF16), b.astype(BF16), preferred_element_type=F32)


def _bmm_nt(a, b):
    return jnp.einsum('cid,cjd->cij', a.astype(BF16), b.astype(BF16), preferred_element_type=F32)


def _unit_lower_inverse(a, ii, jj):
    eye = (ii == jj).astype(F32)
    blk = lambda n: (ii // n) == (jj // n)
    ad = jnp.where(blk(16), a, 0.0)
    t = eye - ad
    p = ad
    for _ in range(3):
        p = _bmm(p, p)
        t = _bmm(t, eye + p)
    for n in (32, 64):
        off = jnp.where(blk(n) & ~blk(n // 2), a, 0.0)
        t = t - _bmm(_bmm(t, off), t)
    return t


def _gdn_kernel(x_ref, sm_ref, z_ref, w_ref, alog_ref, dtb_ref, og_ref, o_ref, sg_ref,
                hist_ref, qkv_scr, s_ref, u_scr, wq_scr, akd_scr, egl_scr, o_scr):
    C, DK, NC, H, R = GDN_CHUNK, GDN_HEAD_DIM, GDN_NC, GDN_HEADS, GDN_ROWS
    BN = R * NC
    ts = NC * C

    @pl.when(pl.program_id(1) == 0)
    def _():
        s_ref[...] = jnp.zeros_like(s_ref)
        hist_ref[...] = jnp.zeros_like(hist_ref)

    w = [w_ref[j:j + 1, :] for j in range(GDN_CONV)]
    for row in range(R):
        x = x_ref[row]
        head, hist = x[:SUBLANES], hist_ref[row]
        y, y_head = x * w[GDN_CONV - 1], head * w[GDN_CONV - 1]
        for k in range(1, GDN_CONV):
            y = y + pltpu.roll(x, k, 0) * w[GDN_CONV - 1 - k]
            y_head = y_head + _shift_rows(head, hist, k) * w[GDN_CONV - 1 - k]
        hist_ref[row] = x[ts - SUBLANES:, :]
        for j, (t, t_head) in enumerate(zip(_gdn_qkv(y), _gdn_qkv(y_head))):
            width = t.shape[1]
            qkv_scr[row * ts:(row + 1) * ts, j * DK:j * DK + width] = t
            qkv_scr[row * ts:row * ts + SUBLANES, j * DK:j * DK + width] = t_head

    sm = sm_ref[...]
    sg = _sigmoid(sm)
    sg_ref[...] = sg
    a = sm + dtb_ref[...]
    softplus = jnp.maximum(a, 0.0) + jnp.log(1.0 + jnp.exp(-jnp.abs(a)))
    gl = -jnp.exp(alog_ref[...]) * softplus

    ii = lax.broadcasted_iota(jnp.int32, (BN, C, C), 1)
    jj = lax.broadcasted_iota(jnp.int32, (BN, C, C), 2)
    causal = ii >= jj

    for h in range(H):
        sl = slice(h * DK, (h + 1) * DK)
        q3 = qkv_scr[:, h * DK:(h + 1) * DK].reshape(BN, C, DK)
        k3 = qkv_scr[:, GDN_WIDTH + h * DK:GDN_WIDTH + (h + 1) * DK].reshape(BN, C, DK)
        v3 = qkv_scr[:, 2 * GDN_WIDTH + h * DK:2 * GDN_WIDTH + (h + 1) * DK].reshape(BN, C, DK)
        g_col = gl[:, :, GDN_A_COL + h:GDN_A_COL + h + 1].reshape(BN, C, 1)
        b_col = sg[:, :, GDN_B_COL + h:GDN_B_COL + h + 1].reshape(BN, C, 1)
        g_mat = jnp.broadcast_to(g_col, (BN, C, C))
        g_row = jnp.sum(jnp.where(ii == jj, g_mat, 0.0), axis=1, keepdims=True)
        gc_col = jnp.sum(jnp.where(causal, jnp.broadcast_to(g_row, (BN, C, C)), 0.0), axis=2, keepdims=True)
        gc_row = jnp.sum(jnp.where(ii <= jj, g_mat, 0.0), axis=1, keepdims=True)
        decay = jnp.where(causal, jnp.exp(jnp.where(causal, gc_col - gc_row, 0.0)), 0.0)
        kb = k3 * b_col
        kq = _bmm_nt(jnp.concatenate([kb, q3], axis=1), k3)
        a_kk = jnp.where(ii > jj, kq[:, :C] * decay, 0.0)
        a_qk = kq[:, C:] * decay
        t_inv = _unit_lower_inverse(a_kk, ii, jj)
        egc = jnp.exp(gc_col)
        uw = _bmm(t_inv, jnp.concatenate([v3 * b_col, kb * egc], axis=2))
        g_last = gc_col[:, C - 1:C, :]
        kd = k3 * jnp.exp(g_last - gc_col)
        u_scr[h] = uw[:, :, :DK]
        wq_scr[h] = jnp.concatenate([uw[:, :, DK:], q3 * egc], axis=1).astype(BF16)
        akd_scr[h] = jnp.concatenate([a_qk, jnp.swapaxes(kd, 1, 2)], axis=1).astype(BF16)
        egl_scr[h] = jnp.broadcast_to(jnp.exp(g_last), (BN, SUBLANES, DK))

    def step(c, carry):
        for row in range(R):
            n = row * NC + c
            r0 = pl.multiple_of(n * C, C)
            for h in range(H):
                s = s_ref[row * H + h]
                r = jnp.dot(wq_scr[h, n], s.astype(BF16), preferred_element_type=F32)
                v_new = (u_scr[h, n] - r[:C]).astype(BF16)
                r2 = jnp.dot(akd_scr[h, n], v_new, preferred_element_type=F32)
                o_scr[pl.ds(r0, C), h * DK:(h + 1) * DK] = r[C:] + r2[:C]
                s_ref[row * H + h] = s * egl_scr[h, n][0:1, :] + r2[C:]
        return carry

    lax.fori_loop(0, NC, step, 0)

    for h in range(H):
        sl = slice(h * DK, (h + 1) * DK)
        o = o_scr[:, sl]
        o = o * lax.rsqrt(jnp.mean(o * o, axis=-1, keepdims=True) + NORM_EPS) * og_ref[...]
        o_ref[:, :, sl] = o.reshape(R, NC * C, DK) * _silu(z_ref[:, :, sl])


def _gdn(gqkv, small, z, conv_w, a_log, dt_bias, out_gain, B, T):
    C, NC, H, DK, R = GDN_CHUNK, GDN_NC, GDN_HEADS, GDN_HEAD_DIM, GDN_ROWS
    ts = NC * C
    assert B % R == 0 and T % ts == 0
    w = jnp.concatenate([conv_w, jnp.zeros((SUBLANES - GDN_CONV, 3 * GDN_WIDTH), F32)], axis=0)
    place = lambda v: jnp.zeros((1, SMALL_WIDTH), F32).at[0, GDN_A_COL:GDN_A_COL + GDN_HEADS].set(v)
    tok = lambda c: pl.BlockSpec((R, ts, c), lambda b, i: (b, i, 0))
    return pl.pallas_call(
        _gdn_kernel,
        grid=(B // R, T // ts),
        in_specs=[tok(3 * GDN_WIDTH), tok(SMALL_WIDTH), tok(GDN_WIDTH), _resident((SUBLANES, 3 * GDN_WIDTH)),
                  _resident((1, SMALL_WIDTH)), _resident((1, SMALL_WIDTH)), _resident((1, GDN_HEAD_DIM))],
        out_specs=[tok(GDN_WIDTH), tok(SMALL_WIDTH)],
        out_shape=[jax.ShapeDtypeStruct((B, T, GDN_WIDTH), F32), jax.ShapeDtypeStruct((B, T, SMALL_WIDTH), F32)],
        scratch_shapes=[pltpu.VMEM((R, SUBLANES, 3 * GDN_WIDTH), F32),
                        pltpu.VMEM((R * ts, 3 * GDN_WIDTH), F32),
                        pltpu.VMEM((R * H, DK, DK), F32),
                        pltpu.VMEM((H, R * NC, C, DK), F32),
                        pltpu.VMEM((H, R * NC, 2 * C, DK), BF16),
                        pltpu.VMEM((H, R * NC, C + DK, C), BF16),
                        pltpu.VMEM((H, R * NC, SUBLANES, DK), F32),
                        pltpu.VMEM((R * ts, GDN_WIDTH), F32)],
        compiler_params=_cparams("parallel", "arbitrary"),
        name="gdn",
    )(gqkv.reshape(B, T, 3 * GDN_WIDTH), small.reshape(B, T, SMALL_WIDTH), z.reshape(B, T, GDN_WIDTH), w,
      place(a_log), place(dt_bias), out_gain.reshape(1, GDN_HEAD_DIM))


def _out_proj_kernel(x_ref, a_ref, b_ref, w_ref, o_ref):
    acc = jnp.dot(a_ref[...].astype(BF16), w_ref[0, :NSA_WIDTH, :], preferred_element_type=F32)
    acc = acc + jnp.dot(b_ref[...].astype(BF16), w_ref[0, NSA_WIDTH:, :], preferred_element_type=F32)
    o_ref[...] = x_ref[...] + acc


def _out_proj(x2, o_nsa, o_gdn, w_all, layer, tm=512):
    m = x2.shape[0]
    tm = min(tm, m)
    row = lambda c: pl.BlockSpec((tm, c), lambda i: (i, 0))
    return pl.pallas_call(
        _out_proj_kernel,
        grid=(m // tm,),
        in_specs=[row(D_MODEL), row(NSA_WIDTH), row(GDN_WIDTH),
                  pl.BlockSpec((1, NSA_WIDTH + GDN_WIDTH, D_MODEL), lambda i: (layer, 0, 0),
                               pipeline_mode=pl.Buffered(1))],
        out_specs=row(D_MODEL),
        out_shape=jax.ShapeDtypeStruct((m, D_MODEL), F32),
        compiler_params=_cparams("parallel"),
        name="out_proj",
    )(x2, o_nsa.reshape(m, NSA_WIDTH), o_gdn.reshape(m, GDN_WIDTH), w_all)


FFN_CHUNK = 256


FFN_HEAD_ROWS = 16


def _ffn_kernel(x_ref, g_ref, wu_ref, cw_ref, cb_ref, wd_ref, o_ref, hist_ref, act_ref):
    @pl.when(pl.program_id(1) == 0)
    def _():
        hist_ref[...] = jnp.zeros_like(hist_ref)

    x = x_ref[0]
    tm = x.shape[0]
    h = (x * lax.rsqrt(jnp.mean(x * x, axis=-1, keepdims=True) + NORM_EPS) * g_ref[...]).astype(BF16)

    def conv_cols(c0):
        cols = slice(c0, c0 + FFN_CHUNK)
        u = jnp.dot(h, wu_ref[0, :, cols], preferred_element_type=F32)
        w = [cw_ref[0, j:j + 1, cols] for j in range(FFN_CONV)]
        b = cb_ref[0, :, cols]
        y = u * w[FFN_CONV - 1] + b
        head = u[:FFN_HEAD_ROWS]
        y_head = head * w[FFN_CONV - 1] + b
        hist = hist_ref[:, cols]
        for k in range(1, FFN_CONV):
            y = y + pltpu.roll(u, k, 0) * w[FFN_CONV - 1 - k]
            y_head = y_head + _shift_rows(head, hist, k) * w[FFN_CONV - 1 - k]
        hist_ref[:, cols] = u[tm - SUBLANES:, :]
        return y, y_head

    for c in range(D_FF // FFN_CHUNK):
        cols = slice(c * FFN_CHUNK, (c + 1) * FFN_CHUNK)
        gate, gate_head = conv_cols(c * FFN_CHUNK)
        up, up_head = conv_cols(D_FF + c * FFN_CHUNK)
        act_ref[:, cols] = (_silu(gate) * up).astype(BF16)
        act_ref[:FFN_HEAD_ROWS, cols] = (_silu(gate_head) * up_head).astype(BF16)
    o_ref[0] = x + jnp.dot(act_ref[...], wd_ref[0], preferred_element_type=F32)


def _ffn(x3, gain, wu_all, cw_all, cb_all, wd_all, layer, tm=512):
    B, T, _ = x3.shape
    tm = min(tm, T)
    lay = lambda s1, s2: pl.BlockSpec((1, s1, s2), lambda b, i: (layer, 0, 0), pipeline_mode=pl.Buffered(1))
    return pl.pallas_call(
        _ffn_kernel,
        grid=(B, T // tm),
        in_specs=[pl.BlockSpec((1, tm, D_MODEL), lambda b, i: (b, i, 0)),
                  _resident((1, D_MODEL)),
                  lay(D_MODEL, 2 * D_FF), lay(SUBLANES, 2 * D_FF), lay(1, 2 * D_FF), lay(D_FF, D_MODEL)],
        out_specs=pl.BlockSpec((1, tm, D_MODEL), lambda b, i: (b, i, 0)),
        out_shape=jax.ShapeDtypeStruct((B, T, D_MODEL), F32),
        scratch_shapes=[pltpu.VMEM((SUBLANES, 2 * D_FF), F32),
                        pltpu.VMEM((tm, D_FF), BF16)],
        compiler_params=_cparams("parallel", "arbitrary"),
        name="ffn",
    )(x3, gain.reshape(1, D_MODEL), wu_all, cw_all, cb_all, wd_all)


def kernel(x, positions, attn_norm, w_in, nsa_q_norm, nsa_k_norm, cmp_pe, cmp_w1, cmp_w2, nsa_out_norm,
           gdn_conv_w, gdn_a_log, gdn_dt_bias, gdn_out_norm, w_out, ffn_norm, w_up, ffn_conv_w,
           ffn_conv_b, w_down):
    B, T, _ = x.shape
    depth = w_in.shape[0]
    M = B * T
    tabs = _rope_tables(positions)
    w_in_r = _regroup_w_in(w_in)
    pe, w1g, w2g = _compress_weights(cmp_pe, cmp_w1, cmp_w2)
    w_out_b = w_out.astype(BF16)
    w_up_b = w_up.astype(BF16)
    w_down_b = w_down.astype(BF16)
    cw = jnp.concatenate([ffn_conv_w, jnp.zeros((depth, SUBLANES - FFN_CONV, 2 * D_FF), F32)], axis=1)
    cb = ffn_conv_b.reshape(depth, 1, 2 * D_FF)

    for l in range(depth):
        x2 = x.reshape(M, D_MODEL)
        q_raw, kv_raw, gqkv, z, small = _in_proj(x2, attn_norm[l], w_in_r, l)
        qt, kc_in, ks, vst, kw, vwt = _nsa_prep(q_raw, kv_raw, tabs, nsa_q_norm[l], nsa_k_norm[l], B, T)
        cmp_t, cmp_n = _compress(kc_in, kv_raw.reshape(B, T, NSA_KV_WIDTH), pe, w1g, w2g,
                                 nsa_k_norm[l, 0], l, B, T)
        o_gdn, sg = _gdn(gqkv, small, z, gdn_conv_w[l], gdn_a_log[l], gdn_dt_bias[l], gdn_out_norm[l], B, T)
        o_nsa = _nsa_attn(qt, sg, cmp_n[:B], cmp_t[B:], ks, vst, kw, vwt, nsa_out_norm[l],
                          nsa_q_norm[l], nsa_k_norm[l], B, T)
        x2 = _out_proj(x2, o_nsa, o_gdn, w_out_b, l)
        x = _ffn(x2.reshape(B, T, D_MODEL), ffn_norm[l], w_up_b, cw, cb, w_down_b, l)
    return x
```

```python
import functools

import numpy as np
import jax
import jax.numpy as jnp
from jax import lax
from jax.experimental import pallas as pl
from jax.experimental.pallas import tpu as pltpu

D_MODEL = 1024
NSA_HEADS = 8
NSA_KV_GROUPS = 2
NSA_HPG = NSA_HEADS // NSA_KV_GROUPS
NSA_HEAD_DIM = 64
N_BRANCH = 3
CMP_BLOCK = 32
CMP_STRIDE = 16
CMP_HIDDEN = 256
SLC_BLOCK = 64
SLC_TOP_N = 16
WINDOW = 512
ROPE_THETA = 500000.0
ROPE_DIM = NSA_HEAD_DIM // 4
GDN_HEADS = 4
GDN_HEAD_DIM = 128
GDN_CONV = 4
GDN_CHUNK = 64
NSA_WIDTH = NSA_HEADS * NSA_HEAD_DIM
GDN_WIDTH = GDN_HEADS * GDN_HEAD_DIM
NSA_KV_WIDTH = N_BRANCH * 2 * NSA_KV_GROUPS * NSA_HEAD_DIM
NSA_GATE_WIDTH = NSA_HEADS * N_BRANCH
D_FF = 2816
FFN_CONV = 3
NORM_EPS = 1e-6

LANES = 128
SUBLANES = 8
SMALL_WIDTH = LANES
GDN_A_COL = NSA_GATE_WIDTH
GDN_B_COL = NSA_GATE_WIDTH + GDN_HEADS
VMEM_LIMIT = 56 * 1024 * 1024
NEG = -1e30
LOG2E = 1.4426950408889634

F32 = jnp.float32
BF16 = jnp.bfloat16


def _cparams(*sem):
    return pltpu.CompilerParams(dimension_semantics=sem, vmem_limit_bytes=VMEM_LIMIT)


def _resident(shape):
    nd = len(shape)
    return pl.BlockSpec(shape, lambda *_: (0,) * nd, pipeline_mode=pl.Buffered(1))


def _silu(x):
    return x * (1.0 / (1.0 + jnp.exp(-x)))


def _sigmoid(x):
    return 1.0 / (1.0 + jnp.exp(-x))


def _rope_kernel(pos_ref, inv_ref, c_ref, s1_ref, s2_ref):
    ang = pos_ref[...].astype(F32) * inv_ref[...]
    lane = lax.broadcasted_iota(jnp.int32, ang.shape, 1) % NSA_HEAD_DIM
    half = ROPE_DIM // 2
    cos = jnp.cos(ang)
    sin = jnp.sin(ang)
    c_ref[...] = jnp.where(lane < ROPE_DIM, cos, 1.0)
    s1_ref[...] = jnp.where((lane >= half) & (lane < ROPE_DIM), sin, 0.0)
    s2_ref[...] = jnp.where(lane < half, -sin, 0.0)


def _rope_tables(positions):
    m = positions.size
    tm = min(m, 1024)
    inv = ROPE_THETA ** (-jnp.arange(0, ROPE_DIM, 2, dtype=F32) / ROPE_DIM)
    lane = np.arange(LANES) % NSA_HEAD_DIM
    inv_row = jnp.where(lane < ROPE_DIM, inv[lane % (ROPE_DIM // 2)], 0.0).reshape(1, LANES)
    tab = jax.ShapeDtypeStruct((m, LANES), F32)
    return pl.pallas_call(
        _rope_kernel,
        grid=(m // tm,),
        in_specs=[pl.BlockSpec((tm, 1), lambda i: (i, 0)), _resident((1, LANES))],
        out_specs=[pl.BlockSpec((tm, LANES), lambda i: (i, 0))] * 3,
        out_shape=[tab] * 3,
        compiler_params=_cparams("parallel"),
        name="rope_tables",
    )(positions.reshape(m, 1), inv_row)


IN_SEG = (NSA_WIDTH, NSA_KV_WIDTH, 3 * GDN_WIDTH, GDN_WIDTH, SMALL_WIDTH)
IN_DTYPES = (BF16, BF16, F32, F32, F32)


def _in_proj_kernel(x_ref, g_ref, w_ref, q_ref, kv_ref, gq_ref, z_ref, sm_ref):
    x = x_ref[...]
    h = x * lax.rsqrt(jnp.mean(x * x, axis=-1, keepdims=True) + NORM_EPS) * g_ref[...]
    h = h.astype(BF16)
    off = 0
    for ref, width in zip((q_ref, kv_ref, gq_ref, z_ref, sm_ref), IN_SEG):
        ref[...] = jnp.dot(h, w_ref[0, :, off:off + width], preferred_element_type=F32).astype(ref.dtype)
        off += width


def _regroup_w_in(w_in):
    o_q = 0
    o_kv = o_q + NSA_WIDTH
    o_g = o_kv + NSA_KV_WIDTH
    o_gq = o_g + NSA_GATE_WIDTH
    o_a = o_gq + 3 * GDN_WIDTH
    o_b = o_a + GDN_HEADS
    o_z = o_b + GDN_HEADS
    pad = SMALL_WIDTH - NSA_GATE_WIDTH - 2 * GDN_HEADS
    parts = [w_in[..., o_q:o_g], w_in[..., o_gq:o_a], w_in[..., o_z:o_z + GDN_WIDTH],
             w_in[..., o_g:o_gq], w_in[..., o_a:o_z],
             jnp.zeros(w_in.shape[:-1] + (pad,), w_in.dtype)]
    return jnp.concatenate(parts, axis=-1).astype(BF16)


def _in_proj(x2, gain, w_all, layer, tm=512):
    m = x2.shape[0]
    tm = min(tm, m)
    npad = sum(IN_SEG)
    outs = [jax.ShapeDtypeStruct((m, w), dt) for w, dt in zip(IN_SEG, IN_DTYPES)]
    return pl.pallas_call(
        _in_proj_kernel,
        grid=(m // tm,),
        in_specs=[pl.BlockSpec((tm, D_MODEL), lambda i: (i, 0)),
                  _resident((1, D_MODEL)),
                  pl.BlockSpec((1, D_MODEL, npad), lambda i: (layer, 0, 0), pipeline_mode=pl.Buffered(1))],
        out_specs=[pl.BlockSpec((tm, w), lambda i: (i, 0)) for w in IN_SEG],
        out_shape=outs,
        compiler_params=_cparams("parallel"),
        name="in_proj",
    )(x2, gain.reshape(1, D_MODEL), w_all)


def _head_rms(x, gain_row):
    lane = lax.broadcasted_iota(jnp.int32, x.shape, 1)
    x2 = x * x
    left = jnp.sum(jnp.where(lane < NSA_HEAD_DIM, x2, 0.0), axis=-1, keepdims=True)
    right = jnp.sum(jnp.where(lane >= NSA_HEAD_DIM, x2, 0.0), axis=-1, keepdims=True)
    ms = jnp.where(lane < NSA_HEAD_DIM, left, right) * (1.0 / NSA_HEAD_DIM)
    return x * lax.rsqrt(ms + NORM_EPS) * gain_row


def _rope(x, c, s1, s2):
    half = ROPE_DIM // 2
    return x * c + pltpu.roll(x, half, 1) * s1 + pltpu.roll(x, LANES - half, 1) * s2


V_ROWS = NSA_HEAD_DIM + 16


def _values_with_ones(v):
    vt = v.T
    tm = v.shape[0]
    extra = jnp.where(lax.broadcasted_iota(jnp.int32, (V_ROWS - NSA_HEAD_DIM, tm), 0) == 0, 1.0, 0.0)
    parts = []
    for g in range(NSA_KV_GROUPS):
        parts += [vt[g * NSA_HEAD_DIM:(g + 1) * NSA_HEAD_DIM], extra]
    return jnp.concatenate(parts, axis=0).astype(BF16)


def _nsa_prep_kernel(q_ref, kv_ref, c_ref, s1_ref, s2_ref, qg_ref, kg_ref,
                     qt_ref, kc_ref, ks_ref, vst_ref, kw_ref, vwt_ref):
    c, s1, s2 = c_ref[...], s1_ref[...], s2_ref[...]
    qg = qg_ref[...]
    tm = c.shape[0]
    for j in range(NSA_WIDTH // LANES):
        x = q_ref[0, :, j * LANES:(j + 1) * LANES].astype(F32)
        y = _rope(_head_rms(x, qg), c, s1, s2) * (NSA_HEAD_DIM ** -0.5) * LOG2E
        qt_ref[0, j * LANES:(j + 1) * LANES, :] = y.T.astype(BF16)
    kv = lambda i: kv_ref[0, :, i * LANES:(i + 1) * LANES].astype(F32)
    kc_ref[0] = _rope(kv(0), c, s1, s2)
    ks = _rope(_head_rms(kv(2), kg_ref[1:2, :]), c, s1, s2)
    lane = lax.broadcasted_iota(jnp.int32, (tm, LANES), 1)
    blk = (pl.program_id(1) * tm + lax.broadcasted_iota(jnp.int32, (tm, LANES), 0)) // SLC_BLOCK
    onehot = jnp.where(lane - NSA_HEAD_DIM == blk, 1.0, 0.0)
    ks_ref[0, :, :LANES] = jnp.where(lane < NSA_HEAD_DIM, ks, onehot).astype(BF16)
    ks_ref[0, :, LANES:] = jnp.where(lane < NSA_HEAD_DIM, pltpu.roll(ks, NSA_HEAD_DIM, 1), onehot).astype(BF16)
    vst_ref[0] = _values_with_ones(kv(3))
    kw_ref[0] = _rope(_head_rms(kv(4), kg_ref[2:3, :]), c, s1, s2).astype(BF16)
    vwt_ref[0] = _values_with_ones(kv(5))


def _nsa_prep(q_raw, kv_raw, tabs, q_gain, k_gain, B, T, tm=512):
    tm = min(tm, T)
    nt = T // tm
    c, s1, s2 = tabs
    qg = jnp.tile(q_gain.reshape(1, NSA_HEAD_DIM), (1, 2))
    kg = jnp.tile(k_gain, (1, 2))
    kg = jnp.concatenate([kg, jnp.zeros((5, LANES), F32)], axis=0)
    tok = lambda w: pl.BlockSpec((1, tm, w), lambda b, i: (b, i, 0))
    tab = pl.BlockSpec((tm, LANES), lambda b, i: (b * nt + i, 0))
    tr = pl.BlockSpec((1, NSA_KV_GROUPS * V_ROWS, tm), lambda b, i: (b, 0, i))
    return pl.pallas_call(
        _nsa_prep_kernel,
        grid=(B, nt),
        in_specs=[tok(NSA_WIDTH), tok(NSA_KV_WIDTH), tab, tab, tab,
                  _resident((1, LANES)), _resident((8, LANES))],
        out_specs=[pl.BlockSpec((1, NSA_WIDTH, tm), lambda b, i: (b, 0, i)),
                   tok(LANES), tok(2 * LANES), tr, tok(LANES), tr],
        out_shape=[jax.ShapeDtypeStruct((B, NSA_WIDTH, T), BF16),
                   jax.ShapeDtypeStruct((B, T, LANES), F32),
                   jax.ShapeDtypeStruct((B, T, 2 * LANES), BF16),
                   jax.ShapeDtypeStruct((B, NSA_KV_GROUPS * V_ROWS, T), BF16),
                   jax.ShapeDtypeStruct((B, T, LANES), BF16),
                   jax.ShapeDtypeStruct((B, NSA_KV_GROUPS * V_ROWS, T), BF16)],
        compiler_params=_cparams("parallel", "parallel"),
        name="nsa_prep",
    )(q_raw.reshape(B, T, NSA_WIDTH), kv_raw.reshape(B, T, NSA_KV_WIDTH), c, s1, s2, qg, kg)


CMP_ROW = CMP_STRIDE * LANES
CMP_HID2 = NSA_KV_GROUPS * CMP_HIDDEN


def _compress_kernel(r_ref, pe_ref, w1_ref, w2_ref, kg_ref, t_ref, n_ref):
    is_k = pl.program_id(0) == 0
    r = r_ref[0, 0]
    ha = jnp.dot((r + pe_ref[0, 0:1, :]).astype(BF16), w1_ref[0, 0], preferred_element_type=F32)
    hb = jnp.dot((r + pe_ref[0, 1:2, :]).astype(BF16), w1_ref[0, 1], preferred_element_type=F32)
    n16 = r.shape[0]
    hid = ha + pltpu.roll(hb, n16 - 1, 0)
    y = jnp.dot(_silu(hid).astype(BF16), w2_ref[0], preferred_element_type=F32)
    y = jnp.where(is_k, _head_rms(y, kg_ref[...]), y)
    t_ref[0] = y.T.astype(BF16)
    n_ref[0] = y.astype(BF16)


def _compress_weights(cmp_pe, cmp_w1, cmp_w2):
    L = cmp_w1.shape[0]
    G, DH, H = NSA_KV_GROUPS, NSA_HEAD_DIM, CMP_HIDDEN
    w1 = cmp_w1.reshape(L, 2, 2, CMP_STRIDE, DH, H)
    eye = jnp.eye(G, dtype=cmp_w1.dtype)
    w1g = jnp.einsum('lkspdh,gf->lkspgdfh', w1, eye)
    w1g = w1g.reshape(L, 2, 2, CMP_ROW, G * H).astype(BF16)
    w2g = jnp.einsum('lkhd,gf->lkghfd', cmp_w2, eye).reshape(L, 2, G * H, G * DH).astype(BF16)
    pe = cmp_pe.reshape(L, 2, 2, CMP_STRIDE, 1, DH)
    pe = jnp.broadcast_to(pe, (L, 2, 2, CMP_STRIDE, G, DH)).reshape(L, 2, 2, CMP_ROW)
    return pe, w1g, w2g


def _compress(kc_in, kv_raw3, pe, w1g, w2g, k_gain0, layer, B, T):
    n16 = T // CMP_STRIDE
    rk = kc_in.reshape(B, n16, CMP_ROW)
    rv = kv_raw3[:, :, LANES:2 * LANES].reshape(B, n16, CMP_ROW)
    r = jnp.stack([rk, rv], axis=0)
    kg = jnp.tile(k_gain0.reshape(1, NSA_HEAD_DIM), (1, 2))
    return pl.pallas_call(
        _compress_kernel,
        grid=(2, B),
        in_specs=[pl.BlockSpec((1, 1, n16, CMP_ROW), lambda s, b: (s, b, 0, 0)),
                  pl.BlockSpec((1, 2, CMP_ROW), lambda s, b: (layer * 2 + s, 0, 0)),
                  pl.BlockSpec((1, 2, CMP_ROW, CMP_HID2), lambda s, b: (layer * 2 + s, 0, 0, 0)),
                  pl.BlockSpec((1, CMP_HID2, LANES), lambda s, b: (layer * 2 + s, 0, 0)),
                  _resident((1, LANES))],
        out_specs=[pl.BlockSpec((1, LANES, n16), lambda s, b: (s * B + b, 0, 0)),
                   pl.BlockSpec((1, n16, LANES), lambda s, b: (s * B + b, 0, 0))],
        out_shape=[jax.ShapeDtypeStruct((2 * B, LANES, n16), BF16),
                   jax.ShapeDtypeStruct((2 * B, n16, LANES), BF16)],
        compiler_params=_cparams("arbitrary", "parallel"),
        name="compress",
    )(r, pe.reshape(-1, 2, CMP_ROW), w1g.reshape(-1, 2, CMP_ROW, CMP_HID2),
      w2g.reshape(-1, CMP_HID2, LANES), kg)


ATT_TQ = 256
ATT_TK = 512
WIN_KEYS = WINDOW + ATT_TQ
SLC_MASK_DIM = 64


def _softmax_cols(s):
    m = jnp.max(s, axis=0, keepdims=True)
    p = jnp.where(m > 0.5 * NEG, jnp.exp2(s - m), 0.0)
    return p, 1.0 / jnp.maximum(jnp.sum(p, axis=0, keepdims=True), 1e-30)


def _flash_update(s, m, acc, v_ones, fixed_max):
    if fixed_max:
        p = jnp.exp2(s - m).astype(BF16)
        return m, acc + jnp.dot(v_ones, p, preferred_element_type=F32)
    m_new = jnp.maximum(m, jnp.max(s, axis=0, keepdims=True))
    p = jnp.exp2(s - m_new).astype(BF16)
    acc = acc * jnp.exp2(m - m_new) + jnp.dot(v_ones, p, preferred_element_type=F32)
    return m_new, acc


def _flash_finish(acc):
    dh = NSA_HEAD_DIM
    return acc[:dh] * (1.0 / jnp.maximum(acc[dh:dh + 1], 1e-30))


def _nsa_attn_kernel(qt_ref, gt_ref, kc_ref, vct_ref, ks_ref, vst_ref, kw_ref, vwt_ref,
                     ovt_ref, ogt_ref, mb_ref, o_ref, *, n_cmp_pad, fixed_max):
    TQ, TK, H, DH, NB = ATT_TQ, ATT_TK, NSA_HPG, NSA_HEAD_DIM, SLC_MASK_DIM
    t0 = pl.program_id(1) * TQ
    heads = lambda a: jnp.concatenate([a] * H, axis=1)
    t_row = t0 + lax.broadcasted_iota(jnp.int32, (1, TQ), 1)

    ci = lax.broadcasted_iota(jnp.int32, (n_cmp_pad, TQ), 0)
    bias_c = heads(jnp.where(ci * CMP_STRIDE + (CMP_BLOCK - 1) <= t_row, 0.0, NEG))
    w0 = pl.multiple_of(jnp.maximum(t0 - WINDOW, 0), LANES)
    rel = t_row - (w0 + lax.broadcasted_iota(jnp.int32, (WIN_KEYS, TQ), 0))
    bias_w = heads(jnp.where((rel >= 0) & (rel < WINDOW), 0.0, NEG))
    kd = t0 // TK
    k0d = pl.multiple_of(kd * TK, TK)
    bias_d = heads(jnp.where(k0d + lax.broadcasted_iota(jnp.int32, (TK, TQ), 0) <= t_row, 0.0, NEG))

    jj = lax.broadcasted_iota(jnp.int32, (NB, TQ), 0)
    cur = jnp.broadcast_to(t_row // SLC_BLOCK, (NB, TQ))
    forced = (jj == 0) | (jj == cur) | (jj == cur - 1)
    j8 = lax.broadcasted_iota(jnp.int32, (SUBLANES, TQ), 0)

    sg_t = gt_ref[0].T
    zeros_q = jnp.zeros((DH, TQ), BF16)
    G = NSA_KV_GROUPS
    rhs_q, rhs_s, o_c = [], [], []

    for g in range(G):
        lo, hi = g * DH, (g + 1) * DH
        q_t = [qt_ref[0, (g * H + h) * DH:(g * H + h + 1) * DH, :] for h in range(H)]
        pad = (lambda x: jnp.concatenate([x, zeros_q], axis=0)) if g == 0 else \
              (lambda x: jnp.concatenate([zeros_q, x], axis=0))
        rhs_q.append(jnp.concatenate([pad(x) for x in q_t], axis=1))

        pc, inv_c = _softmax_cols(jnp.dot(kc_ref[0], rhs_q[g], preferred_element_type=F32) + bias_c)
        o_c.append(jnp.dot(vct_ref[0, lo:hi, :], pc.astype(BF16), preferred_element_type=F32) * inv_c)

        def ranked_mask(nb, pc=pc, inv_c=inv_c):
            psum = pc[:, :TQ] * inv_c[:, :TQ]
            for h in range(1, H):
                psum = psum + pc[:, h * TQ:(h + 1) * TQ] * inv_c[:, h * TQ:(h + 1) * TQ]
            imp = jnp.dot(ovt_ref[...], psum, preferred_element_type=F32, precision=lax.Precision.HIGHEST)
            val = jnp.where(forced, jnp.inf, jnp.where(jj > cur, -jnp.inf, imp))
            vals = [val[SUBLANES * v:SUBLANES * (v + 1)] for v in range(nb // SUBLANES)]
            cnts = [jnp.zeros((SUBLANES, TQ), F32) for _ in range(NB // SUBLANES)]
            for jp in range(nb):
                row = val[jp:jp + 1, :]
                for v in range(len(vals)):
                    if SUBLANES * v > jp:
                        beat = jnp.where(row >= vals[v], 1.0, 0.0)
                    elif SUBLANES * v + SUBLANES - 1 < jp:
                        beat = jnp.where(row > vals[v], 1.0, 0.0)
                    else:
                        beat = jnp.where(j8 > jp - SUBLANES * v, jnp.where(row >= vals[v], 1.0, 0.0),
                                         jnp.where(row > vals[v], 1.0, 0.0))
                    cnts[v] = cnts[v] + beat
            cnt = jnp.concatenate(cnts, axis=0)
            return jnp.where((cnt < float(SLC_TOP_N)) & (jj <= cur), 0.0, NEG).astype(BF16)

        def causal_mask():
            return jnp.where(jj <= cur, 0.0, NEG).astype(BF16)

        sizes = list(range(2 * SLC_TOP_N, NB + 1, SLC_TOP_N))
        branches = [causal_mask] + [functools.partial(ranked_mask, nb) for nb in sizes]
        n_causal = (t0 + TQ) // SLC_BLOCK
        mask_t = lax.switch(jnp.clip((n_causal - 1) // SLC_TOP_N, 0, len(sizes)), branches)

        rhs_s.append(jnp.concatenate([jnp.concatenate([x, mask_t], axis=0) for x in q_t], axis=1))

    def slc_tile(k0, carry, bias):
        out = []
        for g in range(G):
            s = jnp.dot(ks_ref[0, pl.ds(k0, TK), g * LANES:(g + 1) * LANES], rhs_s[g], preferred_element_type=F32)
            if bias is not None:
                s = s + bias
            out.append(_flash_update(s, *carry[g], vst_ref[0, g * V_ROWS:(g + 1) * V_ROWS, pl.ds(k0, TK)],
                                     fixed_max))
        return tuple(out)

    acc0 = jnp.zeros((V_ROWS, H * TQ), F32)
    m0_s = mb_ref[0:1, :] if fixed_max else jnp.full((1, H * TQ), NEG, F32)
    m0_w = mb_ref[1:2, :] if fixed_max else jnp.full((1, H * TQ), NEG, F32)
    carry = lax.fori_loop(0, kd, lambda kt, c: slc_tile(pl.multiple_of(kt * TK, TK), c, None), ((m0_s, acc0),) * G)
    carry = slc_tile(k0d, carry, bias_d)

    out_rows = []
    for g in range(G):
        o_s = _flash_finish(carry[g][1])

        sw = jnp.dot(kw_ref[0, pl.ds(w0, WIN_KEYS), :], rhs_q[g], preferred_element_type=F32) + bias_w
        o_w = _flash_finish(_flash_update(sw, m0_w, acc0, vwt_ref[0, g * V_ROWS:(g + 1) * V_ROWS,
                                                                  pl.ds(w0, WIN_KEYS)], fixed_max)[1])

        def gate(br):
            rows = [(g * H + h) * N_BRANCH + br for h in range(H)]
            return jnp.concatenate([sg_t[r:r + 1, :] for r in rows], axis=1)

        o = gate(0) * o_c[g] + gate(1) * o_s + gate(2) * o_w
        o = o * lax.rsqrt(jnp.mean(o * o, axis=0, keepdims=True) + NORM_EPS)
        for h in range(H):
            out_rows.append(o[:, h * TQ:(h + 1) * TQ] * ogt_ref[:, g * H + h:g * H + h + 1])

    o_ref[0] = jnp.concatenate(out_rows, axis=0).T.astype(o_ref.dtype)


def _nsa_consts(T):
    n_cmp = (T - CMP_BLOCK) // CMP_STRIDE + 1
    n_slc = T // SLC_BLOCK
    n_cmp_pad = T // CMP_STRIDE
    cs = np.arange(n_cmp_pad) * CMP_STRIDE
    ss = np.arange(SLC_MASK_DIM) * SLC_BLOCK
    ov = ((cs[None, :] < ss[:, None] + SLC_BLOCK) & (cs[None, :] + CMP_BLOCK > ss[:, None]))
    ov = ov & (np.arange(n_cmp_pad)[None, :] < n_cmp) & (np.arange(SLC_MASK_DIM)[:, None] < n_slc)
    return jnp.asarray(ov, F32), n_cmp_pad


FIXED_MAX_LIMIT = 40.0


def _score_bound(q_gain, k_gain):
    return 1.01 * LOG2E * (NSA_HEAD_DIM ** 0.5) * jnp.max(jnp.abs(q_gain)) * jnp.max(jnp.abs(k_gain))


def _nsa_attn(qt, gates, kc, vct, ks, vst, kw, vwt, out_gain, q_gain, k_gain, B, T):
    ovt, n_cmp_pad = _nsa_consts(T)
    assert T // SLC_BLOCK <= SLC_MASK_DIM and T % ATT_TK == 0 and T >= WIN_KEYS
    nq = T // ATT_TQ
    ogt = jnp.zeros((NSA_HEAD_DIM, LANES), F32).at[:, :NSA_HEADS].set(out_gain.T)
    bounds = jnp.stack([_score_bound(q_gain, k_gain[1]), _score_bound(q_gain, k_gain[2])])
    mb = jnp.zeros((SUBLANES, NSA_HPG * ATT_TQ), F32).at[:2].set(bounds[:, None])
    per_b = lambda s1, s2: pl.BlockSpec((1, s1, s2), lambda b, i: (b, 0, 0))

    def call(fixed_max):
        return pl.pallas_call(
            functools.partial(_nsa_attn_kernel, n_cmp_pad=n_cmp_pad, fixed_max=fixed_max),
            grid=(B, nq),
            in_specs=[pl.BlockSpec((1, NSA_WIDTH, ATT_TQ), lambda b, i: (b, 0, i)),
                      pl.BlockSpec((1, ATT_TQ, SMALL_WIDTH), lambda b, i: (b, i, 0)),
                      per_b(n_cmp_pad, LANES), per_b(LANES, n_cmp_pad),
                      per_b(T, 2 * LANES), per_b(NSA_KV_GROUPS * V_ROWS, T),
                      per_b(T, LANES), per_b(NSA_KV_GROUPS * V_ROWS, T),
                      _resident((SLC_MASK_DIM, n_cmp_pad)), _resident((NSA_HEAD_DIM, LANES)),
                      _resident((SUBLANES, NSA_HPG * ATT_TQ))],
            out_specs=pl.BlockSpec((1, ATT_TQ, NSA_WIDTH), lambda b, i: (b, i, 0)),
            out_shape=jax.ShapeDtypeStruct((B, T, NSA_WIDTH), BF16),
            compiler_params=_cparams("parallel", "parallel"),
            name="nsa_attn_fixed" if fixed_max else "nsa_attn_online",
        )

    args = (qt, gates, kc, vct, ks, vst, kw, vwt, ovt, ogt, mb)
    return lax.cond(jnp.max(bounds) <= FIXED_MAX_LIMIT, lambda a: call(True)(*a), lambda a: call(False)(*a), args)


def _shift_rows(x, hist, k):
    xr = pltpu.roll(x, k, 0)
    hr = pltpu.roll(hist, k, 0)
    row = lax.broadcasted_iota(jnp.int32, hist.shape, 0)
    top = jnp.where(row < k, hr, xr[:SUBLANES])
    if x.shape[0] == SUBLANES:
        return top
    return jnp.concatenate([top, xr[SUBLANES:]], axis=0)


def _gdn_qkv(y):
    cols = []
    for src, scale in ((0, GDN_HEAD_DIM ** -0.5), (1, None)):
        for h in range(GDN_HEADS):
            c0 = src * GDN_WIDTH + h * GDN_HEAD_DIM
            t = _silu(y[:, c0:c0 + GDN_HEAD_DIM])
            t = t * lax.rsqrt(jnp.sum(t * t, axis=-1, keepdims=True) + NORM_EPS)
            cols.append(t if scale is None else t * scale)
    cols.append(_silu(y[:, 2 * GDN_WIDTH:]))
    return cols


GDN_NC = 8
GDN_ROWS = 2


def _bmm(a, b):
    return jnp.einsum('cij,cjk->cik', a.astype(BF16), b.astype(BF16), preferred_element_type=F32)


def _bmm_nt(a, b):
    return jnp.einsum('cid,cjd->cij', a.astype(BF16), b.astype(BF16), preferred_element_type=F32)


def _unit_lower_inverse(a, ii, jj):
    eye = (ii == jj).astype(F32)
    blk = lambda n: (ii // n) == (jj // n)
    ad = jnp.where(blk(16), a, 0.0)
    t = eye - ad
    p = ad
    for _ in range(3):
        p = _bmm(p, p)
        t = _bmm(t, eye + p)
    for n in (32, 64):
        off = jnp.where(blk(n) & ~blk(n // 2), a, 0.0)
        t = t - _bmm(_bmm(t, off), t)
    return t


def _gdn_kernel(x_ref, sm_ref, z_ref, w_ref, alog_ref, dtb_ref, og_ref, o_ref, sg_ref,
                hist_ref, qkv_scr, s_ref, u_scr, wq_scr, akd_scr, egl_scr, o_scr):
    C, DK, NC, H, R = GDN_CHUNK, GDN_HEAD_DIM, GDN_NC, GDN_HEADS, GDN_ROWS
    BN = R * NC
    ts = NC * C

    @pl.when(pl.program_id(1) == 0)
    def _():
        s_ref[...] = jnp.zeros_like(s_ref)
        hist_ref[...] = jnp.zeros_like(hist_ref)

    w = [w_ref[j:j + 1, :] for j in range(GDN_CONV)]
    for row in range(R):
        x = x_ref[row]
        head, hist = x[:SUBLANES], hist_ref[row]
        y, y_head = x * w[GDN_CONV - 1], head * w[GDN_CONV - 1]
        for k in range(1, GDN_CONV):
            y = y + pltpu.roll(x, k, 0) * w[GDN_CONV - 1 - k]
            y_head = y_head + _shift_rows(head, hist, k) * w[GDN_CONV - 1 - k]
        hist_ref[row] = x[ts - SUBLANES:, :]
        for j, (t, t_head) in enumerate(zip(_gdn_qkv(y), _gdn_qkv(y_head))):
            width = t.shape[1]
            qkv_scr[row * ts:(row + 1) * ts, j * DK:j * DK + width] = t
            qkv_scr[row * ts:row * ts + SUBLANES, j * DK:j * DK + width] = t_head

    sm = sm_ref[...]
    sg = _sigmoid(sm)
    sg_ref[...] = sg
    a = sm + dtb_ref[...]
    softplus = jnp.maximum(a, 0.0) + jnp.log(1.0 + jnp.exp(-jnp.abs(a)))
    gl = -jnp.exp(alog_ref[...]) * softplus

    ii = lax.broadcasted_iota(jnp.int32, (BN, C, C), 1)
    jj = lax.broadcasted_iota(jnp.int32, (BN, C, C), 2)
    causal = ii >= jj

    for h in range(H):
        sl = slice(h * DK, (h + 1) * DK)
        q3 = qkv_scr[:, h * DK:(h + 1) * DK].reshape(BN, C, DK)
        k3 = qkv_scr[:, GDN_WIDTH + h * DK:GDN_WIDTH + (h + 1) * DK].reshape(BN, C, DK)
        v3 = qkv_scr[:, 2 * GDN_WIDTH + h * DK:2 * GDN_WIDTH + (h + 1) * DK].reshape(BN, C, DK)
        g_col = gl[:, :, GDN_A_COL + h:GDN_A_COL + h + 1].reshape(BN, C, 1)
        b_col = sg[:, :, GDN_B_COL + h:GDN_B_COL + h + 1].reshape(BN, C, 1)
        g_mat = jnp.broadcast_to(g_col, (BN, C, C))
        g_row = jnp.sum(jnp.where(ii == jj, g_mat, 0.0), axis=1, keepdims=True)
        gc_col = jnp.sum(jnp.where(causal, jnp.broadcast_to(g_row, (BN, C, C)), 0.0), axis=2, keepdims=True)
        gc_row = jnp.sum(jnp.where(ii <= jj, g_mat, 0.0), axis=1, keepdims=True)
        decay = jnp.where(causal, jnp.exp(jnp.where(causal, gc_col - gc_row, 0.0)), 0.0)
        kb = k3 * b_col
        kq = _bmm_nt(jnp.concatenate([kb, q3], axis=1), k3)
        a_kk = jnp.where(ii > jj, kq[:, :C] * decay, 0.0)
        a_qk = kq[:, C:] * decay
        t_inv = _unit_lower_inverse(a_kk, ii, jj)
        egc = jnp.exp(gc_col)
        uw = _bmm(t_inv, jnp.concatenate([v3 * b_col, kb * egc], axis=2))
        g_last = gc_col[:, C - 1:C, :]
        kd = k3 * jnp.exp(g_last - gc_col)
        u_scr[h] = uw[:, :, :DK]
        wq_scr[h] = jnp.concatenate([uw[:, :, DK:], q3 * egc], axis=1).astype(BF16)
        akd_scr[h] = jnp.concatenate([a_qk, jnp.swapaxes(kd, 1, 2)], axis=1).astype(BF16)
        egl_scr[h] = jnp.broadcast_to(jnp.exp(g_last), (BN, SUBLANES, DK))

    def step(c, carry):
        for row in range(R):
            n = row * NC + c
            r0 = pl.multiple_of(n * C, C)
            for h in range(H):
                s = s_ref[row * H + h]
                r = jnp.dot(wq_scr[h, n], s.astype(BF16), preferred_element_type=F32)
                v_new = (u_scr[h, n] - r[:C]).astype(BF16)
                r2 = jnp.dot(akd_scr[h, n], v_new, preferred_element_type=F32)
                o_scr[pl.ds(r0, C), h * DK:(h + 1) * DK] = r[C:] + r2[:C]
                s_ref[row * H + h] = s * egl_scr[h, n][0:1, :] + r2[C:]
        return carry

    lax.fori_loop(0, NC, step, 0)

    for h in range(H):
        sl = slice(h * DK, (h + 1) * DK)
        o = o_scr[:, sl]
        o = o * lax.rsqrt(jnp.mean(o * o, axis=-1, keepdims=True) + NORM_EPS) * og_ref[...]
        o_ref[:, :, sl] = (o.reshape(R, NC * C, DK) * _silu(z_ref[:, :, sl])).astype(o_ref.dtype)


def _gdn(gqkv, small, z, conv_w, a_log, dt_bias, out_gain, B, T):
    C, NC, H, DK, R = GDN_CHUNK, GDN_NC, GDN_HEADS, GDN_HEAD_DIM, GDN_ROWS
    ts = NC * C
    assert B % R == 0 and T % ts == 0
    w = jnp.concatenate([conv_w, jnp.zeros((SUBLANES - GDN_CONV, 3 * GDN_WIDTH), F32)], axis=0)
    place = lambda v: jnp.zeros((1, SMALL_WIDTH), F32).at[0, GDN_A_COL:GDN_A_COL + GDN_HEADS].set(v)
    tok = lambda c: pl.BlockSpec((R, ts, c), lambda b, i: (b, i, 0))
    return pl.pallas_call(
        _gdn_kernel,
        grid=(B // R, T // ts),
        in_specs=[tok(3 * GDN_WIDTH), tok(SMALL_WIDTH), tok(GDN_WIDTH), _resident((SUBLANES, 3 * GDN_WIDTH)),
                  _resident((1, SMALL_WIDTH)), _resident((1, SMALL_WIDTH)), _resident((1, GDN_HEAD_DIM))],
        out_specs=[tok(GDN_WIDTH), tok(SMALL_WIDTH)],
        out_shape=[jax.ShapeDtypeStruct((B, T, GDN_WIDTH), BF16), jax.ShapeDtypeStruct((B, T, SMALL_WIDTH), F32)],
        scratch_shapes=[pltpu.VMEM((R, SUBLANES, 3 * GDN_WIDTH), F32),
                        pltpu.VMEM((R * ts, 3 * GDN_WIDTH), F32),
                        pltpu.VMEM((R * H, DK, DK), F32),
                        pltpu.VMEM((H, R * NC, C, DK), F32),
                        pltpu.VMEM((H, R * NC, 2 * C, DK), BF16),
                        pltpu.VMEM((H, R * NC, C + DK, C), BF16),
                        pltpu.VMEM((H, R * NC, SUBLANES, DK), F32),
                        pltpu.VMEM((R * ts, GDN_WIDTH), F32)],
        compiler_params=_cparams("parallel", "arbitrary"),
        name="gdn",
    )(gqkv.reshape(B, T, 3 * GDN_WIDTH), small.reshape(B, T, SMALL_WIDTH), z.reshape(B, T, GDN_WIDTH), w,
      place(a_log), place(dt_bias), out_gain.reshape(1, GDN_HEAD_DIM))


def _out_proj_kernel(x_ref, a_ref, b_ref, w_ref, o_ref):
    acc = jnp.dot(a_ref[...].astype(BF16), w_ref[0, :NSA_WIDTH, :], preferred_element_type=F32)
    acc = acc + jnp.dot(b_ref[...].astype(BF16), w_ref[0, NSA_WIDTH:, :], preferred_element_type=F32)
    o_ref[...] = x_ref[...] + acc


def _out_proj(x2, o_nsa, o_gdn, w_all, layer, tm=512):
    m = x2.shape[0]
    tm = min(tm, m)
    row = lambda c: pl.BlockSpec((tm, c), lambda i: (i, 0))
    return pl.pallas_call(
        _out_proj_kernel,
        grid=(m // tm,),
        in_specs=[row(D_MODEL), row(NSA_WIDTH), row(GDN_WIDTH),
                  pl.BlockSpec((1, NSA_WIDTH + GDN_WIDTH, D_MODEL), lambda i: (layer, 0, 0),
                               pipeline_mode=pl.Buffered(1))],
        out_specs=row(D_MODEL),
        out_shape=jax.ShapeDtypeStruct((m, D_MODEL), F32),
        compiler_params=_cparams("parallel"),
        name="out_proj",
    )(x2, o_nsa.reshape(m, NSA_WIDTH), o_gdn.reshape(m, GDN_WIDTH), w_all)


FFN_CHUNK = 256


FFN_HEAD_ROWS = 16


def _ffn_kernel(x_ref, g_ref, wu_ref, cw_ref, cb_ref, wd_ref, o_ref, hist_ref, act_ref):
    @pl.when(pl.program_id(1) == 0)
    def _():
        hist_ref[...] = jnp.zeros_like(hist_ref)

    x = x_ref[0]
    tm = x.shape[0]
    h = (x * lax.rsqrt(jnp.mean(x * x, axis=-1, keepdims=True) + NORM_EPS) * g_ref[...]).astype(BF16)

    def conv_cols(c0):
        cols = slice(c0, c0 + FFN_CHUNK)
        u = jnp.dot(h, wu_ref[0, :, cols], preferred_element_type=F32)
        w = [cw_ref[0, j:j + 1, cols] for j in range(FFN_CONV)]
        b = cb_ref[0, :, cols]
        y = u * w[FFN_CONV - 1] + b
        head = u[:FFN_HEAD_ROWS]
        y_head = head * w[FFN_CONV - 1] + b
        hist = hist_ref[:, cols]
        for k in range(1, FFN_CONV):
            y = y + pltpu.roll(u, k, 0) * w[FFN_CONV - 1 - k]
            y_head = y_head + _shift_rows(head, hist, k) * w[FFN_CONV - 1 - k]
        hist_ref[:, cols] = u[tm - SUBLANES:, :]
        return y, y_head

    for c in range(D_FF // FFN_CHUNK):
        cols = slice(c * FFN_CHUNK, (c + 1) * FFN_CHUNK)
        gate, gate_head = conv_cols(c * FFN_CHUNK)
        up, up_head = conv_cols(D_FF + c * FFN_CHUNK)
        act_ref[:, cols] = (_silu(gate) * up).astype(BF16)
        act_ref[:FFN_HEAD_ROWS, cols] = (_silu(gate_head) * up_head).astype(BF16)
    o_ref[0] = x + jnp.dot(act_ref[...], wd_ref[0], preferred_element_type=F32)


def _ffn(x3, gain, wu_all, cw_all, cb_all, wd_all, layer, tm=512):
    B, T, _ = x3.shape
    tm = min(tm, T)
    lay = lambda s1, s2: pl.BlockSpec((1, s1, s2), lambda b, i: (layer, 0, 0), pipeline_mode=pl.Buffered(1))
    return pl.pallas_call(
        _ffn_kernel,
        grid=(B, T // tm),
        in_specs=[pl.BlockSpec((1, tm, D_MODEL), lambda b, i: (b, i, 0)),
                  _resident((1, D_MODEL)),
                  lay(D_MODEL, 2 * D_FF), lay(SUBLANES, 2 * D_FF), lay(1, 2 * D_FF), lay(D_FF, D_MODEL)],
        out_specs=pl.BlockSpec((1, tm, D_MODEL), lambda b, i: (b, i, 0)),
        out_shape=jax.ShapeDtypeStruct((B, T, D_MODEL), F32),
        scratch_shapes=[pltpu.VMEM((SUBLANES, 2 * D_FF), F32),
                        pltpu.VMEM((tm, D_FF), BF16)],
        compiler_params=_cparams("parallel", "arbitrary"),
        name="ffn",
    )(x3, gain.reshape(1, D_MODEL), wu_all, cw_all, cb_all, wd_all)


def kernel(x, positions, attn_norm, w_in, nsa_q_norm, nsa_k_norm, cmp_pe, cmp_w1, cmp_w2, nsa_out_norm,
           gdn_conv_w, gdn_a_log, gdn_dt_bias, gdn_out_norm, w_out, ffn_norm, w_up, ffn_conv_w,
           ffn_conv_b, w_down):
    B, T, _ = x.shape
    depth = w_in.shape[0]
    M = B * T
    tabs = _rope_tables(positions)
    w_in_r = _regroup_w_in(w_in)
    pe, w1g, w2g = _compress_weights(cmp_pe, cmp_w1, cmp_w2)
    w_out_b = w_out.astype(BF16)
    w_up_b = w_up.astype(BF16)
    w_down_b = w_down.astype(BF16)
    cw = jnp.concatenate([ffn_conv_w, jnp.zeros((depth, SUBLANES - FFN_CONV, 2 * D_FF), F32)], axis=1)
    cb = ffn_conv_b.reshape(depth, 1, 2 * D_FF)

    for l in range(depth):
        x2 = x.reshape(M, D_MODEL)
        q_raw, kv_raw, gqkv, z, small = _in_proj(x2, attn_norm[l], w_in_r, l)
        qt, kc_in, ks, vst, kw, vwt = _nsa_prep(q_raw, kv_raw, tabs, nsa_q_norm[l], nsa_k_norm[l], B, T)
        cmp_t, cmp_n = _compress(kc_in, kv_raw.reshape(B, T, NSA_KV_WIDTH), pe, w1g, w2g,
                                 nsa_k_norm[l, 0], l, B, T)
        o_gdn, sg = _gdn(gqkv, small, z, gdn_conv_w[l], gdn_a_log[l], gdn_dt_bias[l], gdn_out_norm[l], B, T)
        o_nsa = _nsa_attn(qt, sg, cmp_n[:B], cmp_t[B:], ks, vst, kw, vwt, nsa_out_norm[l],
                          nsa_q_norm[l], nsa_k_norm[l], B, T)
        x2 = _out_proj(x2, o_nsa, o_gdn, w_out_b, l)
        x = _ffn(x2.reshape(B, T, D_MODEL), ffn_norm[l], w_up_b, cw, cb, w_down_b, l)
    return x
```

```python
import functools

import numpy as np
import jax
import jax.numpy as jnp
from jax import lax
from jax.experimental import pallas as pl
from jax.experimental.pallas import tpu as pltpu

D_MODEL = 1024
NSA_HEADS = 8
NSA_KV_GROUPS = 2
NSA_HPG = NSA_HEADS // NSA_KV_GROUPS
NSA_HEAD_DIM = 64
N_BRANCH = 3
CMP_BLOCK = 32
CMP_STRIDE = 16
CMP_HIDDEN = 256
SLC_BLOCK = 64
SLC_TOP_N = 16
WINDOW = 512
ROPE_THETA = 500000.0
ROPE_DIM = NSA_HEAD_DIM // 4
GDN_HEADS = 4
GDN_HEAD_DIM = 128
GDN_CONV = 4
GDN_CHUNK = 64
NSA_WIDTH = NSA_HEADS * NSA_HEAD_DIM
GDN_WIDTH = GDN_HEADS * GDN_HEAD_DIM
NSA_KV_WIDTH = N_BRANCH * 2 * NSA_KV_GROUPS * NSA_HEAD_DIM
NSA_GATE_WIDTH = NSA_HEADS * N_BRANCH
D_FF = 2816
FFN_CONV = 3
NORM_EPS = 1e-6

LANES = 128
SUBLANES = 8
SMALL_WIDTH = LANES
GDN_A_COL = NSA_GATE_WIDTH
GDN_B_COL = NSA_GATE_WIDTH + GDN_HEADS
VMEM_LIMIT = 56 * 1024 * 1024
NEG = -1e30
LOG2E = 1.4426950408889634

F32 = jnp.float32
BF16 = jnp.bfloat16


def _cparams(*sem):
    return pltpu.CompilerParams(dimension_semantics=sem, vmem_limit_bytes=VMEM_LIMIT)


def _resident(shape):
    nd = len(shape)
    return pl.BlockSpec(shape, lambda *_: (0,) * nd, pipeline_mode=pl.Buffered(1))


def _silu(x):
    return x * (1.0 / (1.0 + jnp.exp(-x)))


def _sigmoid(x):
    return 1.0 / (1.0 + jnp.exp(-x))


def _rope_kernel(pos_ref, inv_ref, c_ref, s1_ref, s2_ref):
    ang = pos_ref[...].astype(F32) * inv_ref[...]
    lane = lax.broadcasted_iota(jnp.int32, ang.shape, 1) % NSA_HEAD_DIM
    half = ROPE_DIM // 2
    cos = jnp.cos(ang)
    sin = jnp.sin(ang)
    c_ref[...] = jnp.where(lane < ROPE_DIM, cos, 1.0)
    s1_ref[...] = jnp.where((lane >= half) & (lane < ROPE_DIM), sin, 0.0)
    s2_ref[...] = jnp.where(lane < half, -sin, 0.0)


def _rope_tables(positions):
    m = positions.size
    tm = min(m, 1024)
    inv = ROPE_THETA ** (-jnp.arange(0, ROPE_DIM, 2, dtype=F32) / ROPE_DIM)
    lane = np.arange(LANES) % NSA_HEAD_DIM
    inv_row = jnp.where(lane < ROPE_DIM, inv[lane % (ROPE_DIM // 2)], 0.0).reshape(1, LANES)
    tab = jax.ShapeDtypeStruct((m, LANES), F32)
    return pl.pallas_call(
        _rope_kernel,
        grid=(m // tm,),
        in_specs=[pl.BlockSpec((tm, 1), lambda i: (i, 0)), _resident((1, LANES))],
        out_specs=[pl.BlockSpec((tm, LANES), lambda i: (i, 0))] * 3,
        out_shape=[tab] * 3,
        compiler_params=_cparams("parallel"),
        name="rope_tables",
    )(positions.reshape(m, 1), inv_row)


IN_SEG = (NSA_WIDTH, NSA_KV_WIDTH, 3 * GDN_WIDTH, GDN_WIDTH, SMALL_WIDTH)


def _in_proj_kernel(x_ref, g_ref, w_ref, c_ref, s1_ref, s2_ref, qg_ref, kg_ref,
                    qt_ref, kc_ref, vc_ref, ks_ref, vst_ref, kw_ref, vwt_ref, gq_ref, z_ref, sm_ref, *, nt):
    x = x_ref[...]
    tm = x.shape[0]
    h = x * lax.rsqrt(jnp.mean(x * x, axis=-1, keepdims=True) + NORM_EPS) * g_ref[...]
    h = h.astype(BF16)
    offs = np.cumsum((0,) + IN_SEG)
    proj = lambda i: jnp.dot(h, w_ref[0, :, offs[i]:offs[i + 1]], preferred_element_type=F32)
    gq_ref[...] = proj(2)
    z_ref[...] = proj(3)
    sm_ref[...] = proj(4)

    c, s1, s2 = c_ref[...], s1_ref[...], s2_ref[...]
    q = proj(0)
    for j in range(NSA_WIDTH // LANES):
        y = _rope(_head_rms(q[:, j * LANES:(j + 1) * LANES], qg_ref[...]), c, s1, s2)
        qt_ref[0, j * LANES:(j + 1) * LANES, :] = (y * (NSA_HEAD_DIM ** -0.5 * LOG2E)).T.astype(BF16)
    kv_all = proj(1)
    kv = lambda i: kv_all[:, i * LANES:(i + 1) * LANES]
    kc_ref[0] = _rope(kv(0), c, s1, s2)
    vc_ref[0] = kv(1)
    ks = _rope(_head_rms(kv(2), kg_ref[1:2, :]), c, s1, s2)
    lane = lax.broadcasted_iota(jnp.int32, (tm, LANES), 1)
    blk = ((pl.program_id(0) % nt) * tm + lax.broadcasted_iota(jnp.int32, (tm, LANES), 0)) // SLC_BLOCK
    onehot = jnp.where(lane - NSA_HEAD_DIM == blk, 1.0, 0.0)
    ks_ref[0, :, :LANES] = jnp.where(lane < NSA_HEAD_DIM, ks, onehot).astype(BF16)
    ks_ref[0, :, LANES:] = jnp.where(lane < NSA_HEAD_DIM, pltpu.roll(ks, NSA_HEAD_DIM, 1), onehot).astype(BF16)
    vst_ref[0] = _values_with_ones(kv(3))
    kw_ref[0] = _rope(_head_rms(kv(4), kg_ref[2:3, :]), c, s1, s2).astype(BF16)
    vwt_ref[0] = _values_with_ones(kv(5))


def _regroup_w_in(w_in):
    o_q = 0
    o_kv = o_q + NSA_WIDTH
    o_g = o_kv + NSA_KV_WIDTH
    o_gq = o_g + NSA_GATE_WIDTH
    o_a = o_gq + 3 * GDN_WIDTH
    o_b = o_a + GDN_HEADS
    o_z = o_b + GDN_HEADS
    pad = SMALL_WIDTH - NSA_GATE_WIDTH - 2 * GDN_HEADS
    parts = [w_in[..., o_q:o_g], w_in[..., o_gq:o_a], w_in[..., o_z:o_z + GDN_WIDTH],
             w_in[..., o_g:o_gq], w_in[..., o_a:o_z],
             jnp.zeros(w_in.shape[:-1] + (pad,), w_in.dtype)]
    return jnp.concatenate(parts, axis=-1).astype(BF16)


def _in_proj(x2, gain, w_all, layer, tabs, q_gain, k_gain, B, T, tm=512):
    m = x2.shape[0]
    tm = min(tm, T)
    nt = T // tm
    npad = sum(IN_SEG)
    vr = NSA_KV_GROUPS * V_ROWS
    qg = jnp.tile(q_gain.reshape(1, NSA_HEAD_DIM), (1, 2))
    kg = jnp.concatenate([jnp.tile(k_gain, (1, 2)), jnp.zeros((SUBLANES - N_BRANCH, LANES), F32)], axis=0)
    rows = lambda w: pl.BlockSpec((tm, w), lambda i: (i, 0))
    tok = lambda w: pl.BlockSpec((1, tm, w), lambda i: (i // nt, i % nt, 0))
    tr = lambda r: pl.BlockSpec((1, r, tm), lambda i: (i // nt, 0, i % nt))
    return pl.pallas_call(
        functools.partial(_in_proj_kernel, nt=nt),
        grid=(m // tm,),
        in_specs=[rows(D_MODEL), _resident((1, D_MODEL)),
                  pl.BlockSpec((1, D_MODEL, npad), lambda i: (layer, 0, 0), pipeline_mode=pl.Buffered(1)),
                  rows(LANES), rows(LANES), rows(LANES), _resident((1, LANES)), _resident((SUBLANES, LANES))],
        out_specs=[tr(NSA_WIDTH), tok(LANES), tok(LANES), tok(2 * LANES), tr(vr), tok(LANES), tr(vr),
                   rows(3 * GDN_WIDTH), rows(GDN_WIDTH), rows(SMALL_WIDTH)],
        out_shape=[jax.ShapeDtypeStruct((B, NSA_WIDTH, T), BF16),
                   jax.ShapeDtypeStruct((B, T, LANES), F32),
                   jax.ShapeDtypeStruct((B, T, LANES), F32),
                   jax.ShapeDtypeStruct((B, T, 2 * LANES), BF16),
                   jax.ShapeDtypeStruct((B, vr, T), BF16),
                   jax.ShapeDtypeStruct((B, T, LANES), BF16),
                   jax.ShapeDtypeStruct((B, vr, T), BF16),
                   jax.ShapeDtypeStruct((m, 3 * GDN_WIDTH), F32),
                   jax.ShapeDtypeStruct((m, GDN_WIDTH), F32),
                   jax.ShapeDtypeStruct((m, SMALL_WIDTH), F32)],
        compiler_params=_cparams("parallel"),
        name="in_proj",
    )(x2, gain.reshape(1, D_MODEL), w_all, *tabs, qg, kg)


def _head_rms(x, gain_row):
    lane = lax.broadcasted_iota(jnp.int32, x.shape, 1)
    x2 = x * x
    left = jnp.sum(jnp.where(lane < NSA_HEAD_DIM, x2, 0.0), axis=-1, keepdims=True)
    right = jnp.sum(jnp.where(lane >= NSA_HEAD_DIM, x2, 0.0), axis=-1, keepdims=True)
    ms = jnp.where(lane < NSA_HEAD_DIM, left, right) * (1.0 / NSA_HEAD_DIM)
    return x * lax.rsqrt(ms + NORM_EPS) * gain_row


def _rope(x, c, s1, s2):
    half = ROPE_DIM // 2
    return x * c + pltpu.roll(x, half, 1) * s1 + pltpu.roll(x, LANES - half, 1) * s2


V_ROWS = NSA_HEAD_DIM + 16


def _values_with_ones(v):
    vt = v.T
    tm = v.shape[0]
    extra = jnp.where(lax.broadcasted_iota(jnp.int32, (V_ROWS - NSA_HEAD_DIM, tm), 0) == 0, 1.0, 0.0)
    parts = []
    for g in range(NSA_KV_GROUPS):
        parts += [vt[g * NSA_HEAD_DIM:(g + 1) * NSA_HEAD_DIM], extra]
    return jnp.concatenate(parts, axis=0).astype(BF16)


CMP_ROW = CMP_STRIDE * LANES
CMP_HID2 = NSA_KV_GROUPS * CMP_HIDDEN


def _compress_kernel(r_ref, pe_ref, w1_ref, w2_ref, kg_ref, t_ref, n_ref):
    is_k = pl.program_id(0) == 0
    r = r_ref[0, 0]
    ha = jnp.dot((r + pe_ref[0, 0:1, :]).astype(BF16), w1_ref[0, 0], preferred_element_type=F32)
    hb = jnp.dot((r + pe_ref[0, 1:2, :]).astype(BF16), w1_ref[0, 1], preferred_element_type=F32)
    n16 = r.shape[0]
    hid = ha + pltpu.roll(hb, n16 - 1, 0)
    y = jnp.dot(_silu(hid).astype(BF16), w2_ref[0], preferred_element_type=F32)
    y = jnp.where(is_k, _head_rms(y, kg_ref[...]), y)
    t_ref[0] = y.T.astype(BF16)
    n_ref[0] = y.astype(BF16)


def _compress_weights(cmp_pe, cmp_w1, cmp_w2):
    L = cmp_w1.shape[0]
    G, DH, H = NSA_KV_GROUPS, NSA_HEAD_DIM, CMP_HIDDEN
    w1 = cmp_w1.reshape(L, 2, 2, CMP_STRIDE, DH, H)
    eye = jnp.eye(G, dtype=cmp_w1.dtype)
    w1g = jnp.einsum('lkspdh,gf->lkspgdfh', w1, eye)
    w1g = w1g.reshape(L, 2, 2, CMP_ROW, G * H).astype(BF16)
    w2g = jnp.einsum('lkhd,gf->lkghfd', cmp_w2, eye).reshape(L, 2, G * H, G * DH).astype(BF16)
    pe = cmp_pe.reshape(L, 2, 2, CMP_STRIDE, 1, DH)
    pe = jnp.broadcast_to(pe, (L, 2, 2, CMP_STRIDE, G, DH)).reshape(L, 2, 2, CMP_ROW)
    return pe, w1g, w2g


def _compress(kc_in, vc_in, pe, w1g, w2g, k_gain0, layer, B, T):
    n16 = T // CMP_STRIDE
    rk = kc_in.reshape(B, n16, CMP_ROW)
    rv = vc_in.reshape(B, n16, CMP_ROW)
    r = jnp.stack([rk, rv], axis=0)
    kg = jnp.tile(k_gain0.reshape(1, NSA_HEAD_DIM), (1, 2))
    return pl.pallas_call(
        _compress_kernel,
        grid=(2, B),
        in_specs=[pl.BlockSpec((1, 1, n16, CMP_ROW), lambda s, b: (s, b, 0, 0)),
                  pl.BlockSpec((1, 2, CMP_ROW), lambda s, b: (layer * 2 + s, 0, 0)),
                  pl.BlockSpec((1, 2, CMP_ROW, CMP_HID2), lambda s, b: (layer * 2 + s, 0, 0, 0)),
                  pl.BlockSpec((1, CMP_HID2, LANES), lambda s, b: (layer * 2 + s, 0, 0)),
                  _resident((1, LANES))],
        out_specs=[pl.BlockSpec((1, LANES, n16), lambda s, b: (s * B + b, 0, 0)),
                   pl.BlockSpec((1, n16, LANES), lambda s, b: (s * B + b, 0, 0))],
        out_shape=[jax.ShapeDtypeStruct((2 * B, LANES, n16), BF16),
                   jax.ShapeDtypeStruct((2 * B, n16, LANES), BF16)],
        compiler_params=_cparams("arbitrary", "parallel"),
        name="compress",
    )(r, pe.reshape(-1, 2, CMP_ROW), w1g.reshape(-1, 2, CMP_ROW, CMP_HID2),
      w2g.reshape(-1, CMP_HID2, LANES), kg)


ATT_TQ = 256
ATT_TK = 512
WIN_KEYS = WINDOW + ATT_TQ
SLC_MASK_DIM = 64


def _softmax_cols(s):
    m = jnp.max(s, axis=0, keepdims=True)
    p = jnp.where(m > 0.5 * NEG, jnp.exp2(s - m), 0.0)
    return p, 1.0 / jnp.maximum(jnp.sum(p, axis=0, keepdims=True), 1e-30)


def _flash_update(s, m, acc, v_ones, fixed_max):
    if fixed_max:
        p = jnp.exp2(s - m).astype(BF16)
        return m, acc + jnp.dot(v_ones, p, preferred_element_type=F32)
    m_new = jnp.maximum(m, jnp.max(s, axis=0, keepdims=True))
    p = jnp.exp2(s - m_new).astype(BF16)
    acc = acc * jnp.exp2(m - m_new) + jnp.dot(v_ones, p, preferred_element_type=F32)
    return m_new, acc


def _flash_finish(acc):
    dh = NSA_HEAD_DIM
    return acc[:dh] * (1.0 / jnp.maximum(acc[dh:dh + 1], 1e-30))


def _nsa_attn_kernel(qt_ref, gt_ref, kc_ref, vct_ref, ks_ref, vst_ref, kw_ref, vwt_ref,
                     ovt_ref, ogt_ref, mb_ref, o_ref, *, n_cmp_pad, fixed_max):
    TQ, TK, H, DH, NB = ATT_TQ, ATT_TK, NSA_HPG, NSA_HEAD_DIM, SLC_MASK_DIM
    t0 = pl.program_id(1) * TQ
    heads = lambda a: jnp.concatenate([a] * H, axis=1)
    t_row = t0 + lax.broadcasted_iota(jnp.int32, (1, TQ), 1)

    ci = lax.broadcasted_iota(jnp.int32, (n_cmp_pad, TQ), 0)
    bias_c = heads(jnp.where(ci * CMP_STRIDE + (CMP_BLOCK - 1) <= t_row, 0.0, NEG))
    w0 = pl.multiple_of(jnp.maximum(t0 - WINDOW, 0), LANES)
    rel = t_row - (w0 + lax.broadcasted_iota(jnp.int32, (WIN_KEYS, TQ), 0))
    bias_w = heads(jnp.where((rel >= 0) & (rel < WINDOW), 0.0, NEG))
    kd = t0 // TK
    k0d = pl.multiple_of(kd * TK, TK)
    bias_d = heads(jnp.where(k0d + lax.broadcasted_iota(jnp.int32, (TK, TQ), 0) <= t_row, 0.0, NEG))

    jj = lax.broadcasted_iota(jnp.int32, (NB, TQ), 0)
    cur = jnp.broadcast_to(t_row // SLC_BLOCK, (NB, TQ))
    forced = (jj == 0) | (jj == cur) | (jj == cur - 1)
    j8 = lax.broadcasted_iota(jnp.int32, (SUBLANES, TQ), 0)

    sg_t = gt_ref[0].T
    zeros_q = jnp.zeros((DH, TQ), BF16)
    G = NSA_KV_GROUPS
    rhs_q, rhs_s, o_c = [], [], []

    for g in range(G):
        lo, hi = g * DH, (g + 1) * DH
        q_t = [qt_ref[0, (g * H + h) * DH:(g * H + h + 1) * DH, :] for h in range(H)]
        pad = (lambda x: jnp.concatenate([x, zeros_q], axis=0)) if g == 0 else \
              (lambda x: jnp.concatenate([zeros_q, x], axis=0))
        rhs_q.append(jnp.concatenate([pad(x) for x in q_t], axis=1))

        pc, inv_c = _softmax_cols(jnp.dot(kc_ref[0], rhs_q[g], preferred_element_type=F32) + bias_c)
        o_c.append(jnp.dot(vct_ref[0, lo:hi, :], pc.astype(BF16), preferred_element_type=F32) * inv_c)

        def ranked_mask(nb, pc=pc, inv_c=inv_c):
            psum = pc[:, :TQ] * inv_c[:, :TQ]
            for h in range(1, H):
                psum = psum + pc[:, h * TQ:(h + 1) * TQ] * inv_c[:, h * TQ:(h + 1) * TQ]
            imp = jnp.dot(ovt_ref[...], psum, preferred_element_type=F32, precision=lax.Precision.HIGHEST)
            val = jnp.where(forced, jnp.inf, jnp.where(jj > cur, -jnp.inf, imp))
            vals = [val[SUBLANES * v:SUBLANES * (v + 1)] for v in range(nb // SUBLANES)]
            cnts = [jnp.zeros((SUBLANES, TQ), F32) for _ in range(NB // SUBLANES)]
            for jp in range(nb):
                row = val[jp:jp + 1, :]
                for v in range(len(vals)):
                    if SUBLANES * v > jp:
                        beat = jnp.where(row >= vals[v], 1.0, 0.0)
                    elif SUBLANES * v + SUBLANES - 1 < jp:
                        beat = jnp.where(row > vals[v], 1.0, 0.0)
                    else:
                        beat = jnp.where(j8 > jp - SUBLANES * v, jnp.where(row >= vals[v], 1.0, 0.0),
                                         jnp.where(row > vals[v], 1.0, 0.0))
                    cnts[v] = cnts[v] + beat
            cnt = jnp.concatenate(cnts, axis=0)
            return jnp.where((cnt < float(SLC_TOP_N)) & (jj <= cur), 0.0, NEG).astype(BF16)

        def causal_mask():
            return jnp.where(jj <= cur, 0.0, NEG).astype(BF16)

        sizes = list(range(2 * SLC_TOP_N, NB + 1, SLC_TOP_N))
        branches = [causal_mask] + [functools.partial(ranked_mask, nb) for nb in sizes]
        n_causal = (t0 + TQ) // SLC_BLOCK
        mask_t = lax.switch(jnp.clip((n_causal - 1) // SLC_TOP_N, 0, len(sizes)), branches)

        rhs_s.append(jnp.concatenate([jnp.concatenate([x, mask_t], axis=0) for x in q_t], axis=1))

    def slc_tile(k0, carry, bias):
        out = []
        for g in range(G):
            s = jnp.dot(ks_ref[0, pl.ds(k0, TK), g * LANES:(g + 1) * LANES], rhs_s[g], preferred_element_type=F32)
            if bias is not None:
                s = s + bias
            out.append(_flash_update(s, *carry[g], vst_ref[0, g * V_ROWS:(g + 1) * V_ROWS, pl.ds(k0, TK)],
                                     fixed_max))
        return tuple(out)

    acc0 = jnp.zeros((V_ROWS, H * TQ), F32)
    m0_s = mb_ref[0:1, :] if fixed_max else jnp.full((1, H * TQ), NEG, F32)
    m0_w = mb_ref[1:2, :] if fixed_max else jnp.full((1, H * TQ), NEG, F32)
    carry = lax.fori_loop(0, kd, lambda kt, c: slc_tile(pl.multiple_of(kt * TK, TK), c, None), ((m0_s, acc0),) * G)
    carry = slc_tile(k0d, carry, bias_d)

    out_rows = []
    for g in range(G):
        o_s = _flash_finish(carry[g][1])

        sw = jnp.dot(kw_ref[0, pl.ds(w0, WIN_KEYS), :], rhs_q[g], preferred_element_type=F32) + bias_w
        o_w = _flash_finish(_flash_update(sw, m0_w, acc0, vwt_ref[0, g * V_ROWS:(g + 1) * V_ROWS,
                                                                  pl.ds(w0, WIN_KEYS)], fixed_max)[1])

        def gate(br):
            rows = [(g * H + h) * N_BRANCH + br for h in range(H)]
            return jnp.concatenate([sg_t[r:r + 1, :] for r in rows], axis=1)

        o = gate(0) * o_c[g] + gate(1) * o_s + gate(2) * o_w
        o = o * lax.rsqrt(jnp.mean(o * o, axis=0, keepdims=True) + NORM_EPS)
        for h in range(H):
            out_rows.append(o[:, h * TQ:(h + 1) * TQ] * ogt_ref[:, g * H + h:g * H + h + 1])

    o_ref[0] = jnp.concatenate(out_rows, axis=0).T.astype(o_ref.dtype)


def _nsa_consts(T):
    n_cmp = (T - CMP_BLOCK) // CMP_STRIDE + 1
    n_slc = T // SLC_BLOCK
    n_cmp_pad = T // CMP_STRIDE
    cs = np.arange(n_cmp_pad) * CMP_STRIDE
    ss = np.arange(SLC_MASK_DIM) * SLC_BLOCK
    ov = ((cs[None, :] < ss[:, None] + SLC_BLOCK) & (cs[None, :] + CMP_BLOCK > ss[:, None]))
    ov = ov & (np.arange(n_cmp_pad)[None, :] < n_cmp) & (np.arange(SLC_MASK_DIM)[:, None] < n_slc)
    return jnp.asarray(ov, F32), n_cmp_pad


FIXED_MAX_LIMIT = 40.0


def _score_bound(q_gain, k_gain):
    return 1.01 * LOG2E * (NSA_HEAD_DIM ** 0.5) * jnp.max(jnp.abs(q_gain)) * jnp.max(jnp.abs(k_gain))


def _nsa_attn(qt, gates, kc, vct, ks, vst, kw, vwt, out_gain, q_gain, k_gain, B, T):
    ovt, n_cmp_pad = _nsa_consts(T)
    assert T // SLC_BLOCK <= SLC_MASK_DIM and T % ATT_TK == 0 and T >= WIN_KEYS
    nq = T // ATT_TQ
    ogt = jnp.zeros((NSA_HEAD_DIM, LANES), F32).at[:, :NSA_HEADS].set(out_gain.T)
    bounds = jnp.stack([_score_bound(q_gain, k_gain[1]), _score_bound(q_gain, k_gain[2])])
    mb = jnp.zeros((SUBLANES, NSA_HPG * ATT_TQ), F32).at[:2].set(bounds[:, None])
    per_b = lambda s1, s2: pl.BlockSpec((1, s1, s2), lambda b, i: (b, 0, 0))

    def call(fixed_max):
        return pl.pallas_call(
            functools.partial(_nsa_attn_kernel, n_cmp_pad=n_cmp_pad, fixed_max=fixed_max),
            grid=(B, nq),
            in_specs=[pl.BlockSpec((1, NSA_WIDTH, ATT_TQ), lambda b, i: (b, 0, i)),
                      pl.BlockSpec((1, ATT_TQ, SMALL_WIDTH), lambda b, i: (b, i, 0)),
                      per_b(n_cmp_pad, LANES), per_b(LANES, n_cmp_pad),
                      per_b(T, 2 * LANES), per_b(NSA_KV_GROUPS * V_ROWS, T),
                      per_b(T, LANES), per_b(NSA_KV_GROUPS * V_ROWS, T),
                      _resident((SLC_MASK_DIM, n_cmp_pad)), _resident((NSA_HEAD_DIM, LANES)),
                      _resident((SUBLANES, NSA_HPG * ATT_TQ))],
            out_specs=pl.BlockSpec((1, ATT_TQ, NSA_WIDTH), lambda b, i: (b, i, 0)),
            out_shape=jax.ShapeDtypeStruct((B, T, NSA_WIDTH), BF16),
            compiler_params=_cparams("parallel", "parallel"),
            name="nsa_attn_fixed" if fixed_max else "nsa_attn_online",
        )

    args = (qt, gates, kc, vct, ks, vst, kw, vwt, ovt, ogt, mb)
    return lax.cond(jnp.max(bounds) <= FIXED_MAX_LIMIT, lambda a: call(True)(*a), lambda a: call(False)(*a), args)


def _shift_rows(x, hist, k):
    xr = pltpu.roll(x, k, 0)
    hr = pltpu.roll(hist, k, 0)
    row = lax.broadcasted_iota(jnp.int32, hist.shape, 0)
    top = jnp.where(row < k, hr, xr[:SUBLANES])
    if x.shape[0] == SUBLANES:
        return top
    return jnp.concatenate([top, xr[SUBLANES:]], axis=0)


def _gdn_qkv(y):
    cols = []
    for src, scale in ((0, GDN_HEAD_DIM ** -0.5), (1, None)):
        for h in range(GDN_HEADS):
            c0 = src * GDN_WIDTH + h * GDN_HEAD_DIM
            t = _silu(y[:, c0:c0 + GDN_HEAD_DIM])
            t = t * lax.rsqrt(jnp.sum(t * t, axis=-1, keepdims=True) + NORM_EPS)
            cols.append(t if scale is None else t * scale)
    cols.append(_silu(y[:, 2 * GDN_WIDTH:]))
    return cols


GDN_NC = 8
GDN_ROWS = 2


def _bmm(a, b):
    return jnp.einsum('cij,cjk->cik', a.astype(BF16), b.astype(BF16), preferred_element_type=F32)


def _bmm_nt(a, b):
    return jnp.einsum('cid,cjd->cij', a.astype(BF16), b.astype(BF16), preferred_element_type=F32)


def _unit_lower_inverse(a, ii, jj):
    eye = (ii == jj).astype(F32)
    blk = lambda n: (ii // n) == (jj // n)
    ad = jnp.where(blk(16), a, 0.0)
    t = eye - ad
    p = ad
    for _ in range(3):
        p = _bmm(p, p)
        t = _bmm(t, eye + p)
    for n in (32, 64):
        off = jnp.where(blk(n) & ~blk(n // 2), a, 0.0)
        t = t - _bmm(_bmm(t, off), t)
    return t


def _gdn_kernel(x_ref, sm_ref, z_ref, w_ref, alog_ref, dtb_ref, og_ref, o_ref, sg_ref,
                hist_ref, qkv_scr, s_ref, u_scr, wq_scr, akd_scr, egl_scr, o_scr):
    C, DK, NC, H, R = GDN_CHUNK, GDN_HEAD_DIM, GDN_NC, GDN_HEADS, GDN_ROWS
    BN = R * NC
    ts = NC * C

    @pl.when(pl.program_id(1) == 0)
    def _():
        s_ref[...] = jnp.zeros_like(s_ref)
        hist_ref[...] = jnp.zeros_like(hist_ref)

    w = [w_ref[j:j + 1, :] for j in range(GDN_CONV)]
    for row in range(R):
        x = x_ref[row]
        head, hist = x[:SUBLANES], hist_ref[row]
        y, y_head = x * w[GDN_CONV - 1], head * w[GDN_CONV - 1]
        for k in range(1, GDN_CONV):
            y = y + pltpu.roll(x, k, 0) * w[GDN_CONV - 1 - k]
            y_head = y_head + _shift_rows(head, hist, k) * w[GDN_CONV - 1 - k]
        hist_ref[row] = x[ts - SUBLANES:, :]
        for j, (t, t_head) in enumerate(zip(_gdn_qkv(y), _gdn_qkv(y_head))):
            width = t.shape[1]
            qkv_scr[row * ts:(row + 1) * ts, j * DK:j * DK + width] = t
            qkv_scr[row * ts:row * ts + SUBLANES, j * DK:j * DK + width] = t_head

    sm = sm_ref[...]
    sg = _sigmoid(sm)
    sg_ref[...] = sg
    a = sm + dtb_ref[...]
    softplus = jnp.maximum(a, 0.0) + jnp.log(1.0 + jnp.exp(-jnp.abs(a)))
    gl = -jnp.exp(alog_ref[...]) * softplus

    ii = lax.broadcasted_iota(jnp.int32, (BN, C, C), 1)
    jj = lax.broadcasted_iota(jnp.int32, (BN, C, C), 2)
    causal = ii >= jj

    for h in range(H):
        sl = slice(h * DK, (h + 1) * DK)
        q3 = qkv_scr[:, h * DK:(h + 1) * DK].reshape(BN, C, DK)
        k3 = qkv_scr[:, GDN_WIDTH + h * DK:GDN_WIDTH + (h + 1) * DK].reshape(BN, C, DK)
        v3 = qkv_scr[:, 2 * GDN_WIDTH + h * DK:2 * GDN_WIDTH + (h + 1) * DK].reshape(BN, C, DK)
        g_col = gl[:, :, GDN_A_COL + h:GDN_A_COL + h + 1].reshape(BN, C, 1)
        b_col = sg[:, :, GDN_B_COL + h:GDN_B_COL + h + 1].reshape(BN, C, 1)
        g_mat = jnp.broadcast_to(g_col, (BN, C, C))
        g_row = jnp.sum(jnp.where(ii == jj, g_mat, 0.0), axis=1, keepdims=True)
        gc_col = jnp.sum(jnp.where(causal, jnp.broadcast_to(g_row, (BN, C, C)), 0.0), axis=2, keepdims=True)
        gc_row = jnp.sum(jnp.where(ii <= jj, g_mat, 0.0), axis=1, keepdims=True)
        decay = jnp.where(causal, jnp.exp(jnp.where(causal, gc_col - gc_row, 0.0)), 0.0)
        kb = k3 * b_col
        kq = _bmm_nt(jnp.concatenate([kb, q3], axis=1), k3)
        a_kk = jnp.where(ii > jj, kq[:, :C] * decay, 0.0)
        a_qk = kq[:, C:] * decay
        t_inv = _unit_lower_inverse(a_kk, ii, jj)
        egc = jnp.exp(gc_col)
        uw = _bmm(t_inv, jnp.concatenate([v3 * b_col, kb * egc], axis=2))
        g_last = gc_col[:, C - 1:C, :]
        kd = k3 * jnp.exp(g_last - gc_col)
        u_scr[h] = uw[:, :, :DK]
        wq_scr[h] = jnp.concatenate([uw[:, :, DK:], q3 * egc], axis=1).astype(BF16)
        akd_scr[h] = jnp.concatenate([a_qk, jnp.swapaxes(kd, 1, 2)], axis=1).astype(BF16)
        egl_scr[h] = jnp.broadcast_to(jnp.exp(g_last), (BN, SUBLANES, DK))

    def step(c, carry):
        for row in range(R):
            n = row * NC + c
            r0 = pl.multiple_of(n * C, C)
            for h in range(H):
                s = s_ref[row * H + h]
                r = jnp.dot(wq_scr[h, n], s.astype(BF16), preferred_element_type=F32)
                v_new = (u_scr[h, n] - r[:C]).astype(BF16)
                r2 = jnp.dot(akd_scr[h, n], v_new, preferred_element_type=F32)
                o_scr[pl.ds(r0, C), h * DK:(h + 1) * DK] = r[C:] + r2[:C]
                s_ref[row * H + h] = s * egl_scr[h, n][0:1, :] + r2[C:]
        return carry

    lax.fori_loop(0, NC, step, 0)

    for h in range(H):
        sl = slice(h * DK, (h + 1) * DK)
        o = o_scr[:, sl]
        o = o * lax.rsqrt(jnp.mean(o * o, axis=-1, keepdims=True) + NORM_EPS) * og_ref[...]
        o_ref[:, :, sl] = (o.reshape(R, NC * C, DK) * _silu(z_ref[:, :, sl])).astype(o_ref.dtype)


def _gdn(gqkv, small, z, conv_w, a_log, dt_bias, out_gain, B, T):
    C, NC, H, DK, R = GDN_CHUNK, GDN_NC, GDN_HEADS, GDN_HEAD_DIM, GDN_ROWS
    ts = NC * C
    assert B % R == 0 and T % ts == 0
    w = jnp.concatenate([conv_w, jnp.zeros((SUBLANES - GDN_CONV, 3 * GDN_WIDTH), F32)], axis=0)
    place = lambda v: jnp.zeros((1, SMALL_WIDTH), F32).at[0, GDN_A_COL:GDN_A_COL + GDN_HEADS].set(v)
    tok = lambda c: pl.BlockSpec((R, ts, c), lambda b, i: (b, i, 0))
    return pl.pallas_call(
        _gdn_kernel,
        grid=(B // R, T // ts),
        in_specs=[tok(3 * GDN_WIDTH), tok(SMALL_WIDTH), tok(GDN_WIDTH), _resident((SUBLANES, 3 * GDN_WIDTH)),
                  _resident((1, SMALL_WIDTH)), _resident((1, SMALL_WIDTH)), _resident((1, GDN_HEAD_DIM))],
        out_specs=[tok(GDN_WIDTH), tok(SMALL_WIDTH)],
        out_shape=[jax.ShapeDtypeStruct((B, T, GDN_WIDTH), BF16), jax.ShapeDtypeStruct((B, T, SMALL_WIDTH), F32)],
        scratch_shapes=[pltpu.VMEM((R, SUBLANES, 3 * GDN_WIDTH), F32),
                        pltpu.VMEM((R * ts, 3 * GDN_WIDTH), F32),
                        pltpu.VMEM((R * H, DK, DK), F32),
                        pltpu.VMEM((H, R * NC, C, DK), F32),
                        pltpu.VMEM((H, R * NC, 2 * C, DK), BF16),
                        pltpu.VMEM((H, R * NC, C + DK, C), BF16),
                        pltpu.VMEM((H, R * NC, SUBLANES, DK), F32),
                        pltpu.VMEM((R * ts, GDN_WIDTH), F32)],
        compiler_params=_cparams("parallel", "arbitrary"),
        name="gdn",
    )(gqkv.reshape(B, T, 3 * GDN_WIDTH), small.reshape(B, T, SMALL_WIDTH), z.reshape(B, T, GDN_WIDTH), w,
      place(a_log), place(dt_bias), out_gain.reshape(1, GDN_HEAD_DIM))


def _out_proj_kernel(x_ref, a_ref, b_ref, w_ref, o_ref):
    acc = jnp.dot(a_ref[...].astype(BF16), w_ref[0, :NSA_WIDTH, :], preferred_element_type=F32)
    acc = acc + jnp.dot(b_ref[...].astype(BF16), w_ref[0, NSA_WIDTH:, :], preferred_element_type=F32)
    o_ref[...] = x_ref[...] + acc


def _out_proj(x2, o_nsa, o_gdn, w_all, layer, tm=512):
    m = x2.shape[0]
    tm = min(tm, m)
    row = lambda c: pl.BlockSpec((tm, c), lambda i: (i, 0))
    return pl.pallas_call(
        _out_proj_kernel,
        grid=(m // tm,),
        in_specs=[row(D_MODEL), row(NSA_WIDTH), row(GDN_WIDTH),
                  pl.BlockSpec((1, NSA_WIDTH + GDN_WIDTH, D_MODEL), lambda i: (layer, 0, 0),
                               pipeline_mode=pl.Buffered(1))],
        out_specs=row(D_MODEL),
        out_shape=jax.ShapeDtypeStruct((m, D_MODEL), F32),
        compiler_params=_cparams("parallel"),
        name="out_proj",
    )(x2, o_nsa.reshape(m, NSA_WIDTH), o_gdn.reshape(m, GDN_WIDTH), w_all)


FFN_CHUNK = 256


FFN_HEAD_ROWS = 16


def _ffn_kernel(x_ref, g_ref, wu_ref, cw_ref, cb_ref, wd_ref, o_ref, hist_ref, act_ref):
    @pl.when(pl.program_id(1) == 0)
    def _():
        hist_ref[...] = jnp.zeros_like(hist_ref)

    x = x_ref[0]
    tm = x.shape[0]
    h = (x * lax.rsqrt(jnp.mean(x * x, axis=-1, keepdims=True) + NORM_EPS) * g_ref[...]).astype(BF16)

    def conv_cols(c0):
        cols = slice(c0, c0 + FFN_CHUNK)
        u = jnp.dot(h, wu_ref[0, :, cols], preferred_element_type=F32)
        w = [cw_ref[0, j:j + 1, cols] for j in range(FFN_CONV)]
        b = cb_ref[0, :, cols]
        y = u * w[FFN_CONV - 1] + b
        head = u[:FFN_HEAD_ROWS]
        y_head = head * w[FFN_CONV - 1] + b
        hist = hist_ref[:, cols]
        for k in range(1, FFN_CONV):
            y = y + pltpu.roll(u, k, 0) * w[FFN_CONV - 1 - k]
            y_head = y_head + _shift_rows(head, hist, k) * w[FFN_CONV - 1 - k]
        hist_ref[:, cols] = u[tm - SUBLANES:, :]
        return y, y_head

    for c in range(D_FF // FFN_CHUNK):
        cols = slice(c * FFN_CHUNK, (c + 1) * FFN_CHUNK)
        gate, gate_head = conv_cols(c * FFN_CHUNK)
        up, up_head = conv_cols(D_FF + c * FFN_CHUNK)
        act_ref[:, cols] = (_silu(gate) * up).astype(BF16)
        act_ref[:FFN_HEAD_ROWS, cols] = (_silu(gate_head) * up_head).astype(BF16)
    o_ref[0] = x + jnp.dot(act_ref[...], wd_ref[0], preferred_element_type=F32)


def _ffn(x3, gain, wu_all, cw_all, cb_all, wd_all, layer, tm=512):
    B, T, _ = x3.shape
    tm = min(tm, T)
    lay = lambda s1, s2: pl.BlockSpec((1, s1, s2), lambda b, i: (layer, 0, 0), pipeline_mode=pl.Buffered(1))
    return pl.pallas_call(
        _ffn_kernel,
        grid=(B, T // tm),
        in_specs=[pl.BlockSpec((1, tm, D_MODEL), lambda b, i: (b, i, 0)),
                  _resident((1, D_MODEL)),
                  lay(D_MODEL, 2 * D_FF), lay(SUBLANES, 2 * D_FF), lay(1, 2 * D_FF), lay(D_FF, D_MODEL)],
        out_specs=pl.BlockSpec((1, tm, D_MODEL), lambda b, i: (b, i, 0)),
        out_shape=jax.ShapeDtypeStruct((B, T, D_MODEL), F32),
        scratch_shapes=[pltpu.VMEM((SUBLANES, 2 * D_FF), F32),
                        pltpu.VMEM((tm, D_FF), BF16)],
        compiler_params=_cparams("parallel", "arbitrary"),
        name="ffn",
    )(x3, gain.reshape(1, D_MODEL), wu_all, cw_all, cb_all, wd_all)


def kernel(x, positions, attn_norm, w_in, nsa_q_norm, nsa_k_norm, cmp_pe, cmp_w1, cmp_w2, nsa_out_norm,
           gdn_conv_w, gdn_a_log, gdn_dt_bias, gdn_out_norm, w_out, ffn_norm, w_up, ffn_conv_w,
           ffn_conv_b, w_down):
    B, T, _ = x.shape
    depth = w_in.shape[0]
    M = B * T
    tabs = _rope_tables(positions)
    w_in_r = _regroup_w_in(w_in)
    pe, w1g, w2g = _compress_weights(cmp_pe, cmp_w1, cmp_w2)
    w_out_b = w_out.astype(BF16)
    w_up_b = w_up.astype(BF16)
    w_down_b = w_down.astype(BF16)
    cw = jnp.concatenate([ffn_conv_w, jnp.zeros((depth, SUBLANES - FFN_CONV, 2 * D_FF), F32)], axis=1)
    cb = ffn_conv_b.reshape(depth, 1, 2 * D_FF)

    for l in range(depth):
        x2 = x.reshape(M, D_MODEL)
        qt, kc_in, vc_in, ks, vst, kw, vwt, gqkv, z, small = _in_proj(
            x2, attn_norm[l], w_in_r, l, tabs, nsa_q_norm[l], nsa_k_norm[l], B, T)
        cmp_t, cmp_n = _compress(kc_in, vc_in, pe, w1g, w2g, nsa_k_norm[l, 0], l, B, T)
        o_gdn, sg = _gdn(gqkv, small, z, gdn_conv_w[l], gdn_a_log[l], gdn_dt_bias[l], gdn_out_norm[l], B, T)
        o_nsa = _nsa_attn(qt, sg, cmp_n[:B], cmp_t[B:], ks, vst, kw, vwt, nsa_out_norm[l],
                          nsa_q_norm[l], nsa_k_norm[l], B, T)
        x2 = _out_proj(x2, o_nsa, o_gdn, w_out_b, l)
        x = _ffn(x2.reshape(B, T, D_MODEL), ffn_norm[l], w_up_b, cw, cb, w_down_b, l)
    return x
```
